```python
import math
import jax, jax.numpy as jnp
from jax import lax
import numpy as np

D_MODEL = 1024
BATCH = 8
SEQ = 2048
DEPTH = 1

HEAD_DIM = 64
MIX_WIDTH = D_MODEL
DIFF_WIDTH = MIX_WIDTH // 2
SB_WIDTH = MIX_WIDTH - DIFF_WIDTH
N_DIFF_HEADS = DIFF_WIDTH // (2 * HEAD_DIM)
DIFF_V_DIM = 2 * HEAD_DIM
N_SB_HEADS = SB_WIDTH // HEAD_DIM
QBLOCK = 128
NUM_BUCKETS = 32
MAX_DISTANCE = 128
N_EXPERTS = 32
TOP_K = 4
D_EXPERT = D_MODEL
SWIGLU_LIMIT = 7.0
SWIGLU_ALPHA = 1.702
PLE_DIM = 256
NORM_EPS = 1e-6
SPLIT_SIZES = (N_DIFF_HEADS * HEAD_DIM, N_DIFF_HEADS * HEAD_DIM, N_DIFF_HEADS * HEAD_DIM,
               N_DIFF_HEADS * HEAD_DIM, N_DIFF_HEADS * DIFF_V_DIM,
               N_SB_HEADS * HEAD_DIM, N_SB_HEADS * HEAD_DIM, N_SB_HEADS * HEAD_DIM)
IN_WIDTH = sum(SPLIT_SIZES)

kernel_name = "hybrid_diffattn_stickbreak_moe_ple"


def rms_norm(x, g):
    x32 = x.astype(jnp.float32)
    y = x32 * lax.rsqrt(jnp.mean(x32 * x32, axis=-1, keepdims=True) + NORM_EPS)
    return (y * g.astype(jnp.float32)).astype(x.dtype)


def t5_bucket(rel):
    n = jnp.maximum(rel, 0)
    max_exact = NUM_BUCKETS // 2
    nf = jnp.maximum(n, 1).astype(jnp.float32)
    large = max_exact + (jnp.log(nf / max_exact) / math.log(MAX_DISTANCE / max_exact)
                         * (NUM_BUCKETS - max_exact)).astype(jnp.int32)
    large = jnp.minimum(large, NUM_BUCKETS - 1)
    return jnp.where(n < max_exact, n, large)


def to_query_blocks(t):
    b, s, h, d = t.shape
    return t.reshape(b, s // QBLOCK, QBLOCK, h, d).transpose(1, 0, 3, 2, 4)


def from_query_blocks(t):
    nb, b, h, blk, d = t.shape
    return t.transpose(1, 0, 3, 2, 4).reshape(b, nb * blk, h * d)


def diff_attention(q1, q2, k1, k2, v, lam, subln_g, lambda_init, rel_bias):
    seq = q1.shape[1]
    scale = HEAD_DIM ** -0.5
    k1h, k2h, vh = [t.transpose(0, 2, 1, 3) for t in (k1, k2, v)]
    kpos = jnp.arange(seq)
    bias_table = rel_bias.astype(jnp.float32)

    def block(args):
        q1b, q2b, qpos = args
        rel = qpos[:, None] - kpos[None, :]
        causal = rel >= 0
        bias = bias_table[t5_bucket(rel)].transpose(2, 0, 1)

        def softmax_map(qb, kh):
            s = jnp.einsum('bhqd,bhkd->bhqk', qb, kh).astype(jnp.float32) * scale + bias
            s = jnp.where(causal, s, -jnp.inf)
            return jax.nn.softmax(s, axis=-1)

        a = softmax_map(q1b, k1h) - lam * softmax_map(q2b, k2h)
        o = jnp.einsum('bhqk,bhkd->bhqd', a.astype(vh.dtype), vh)
        return rms_norm(o, subln_g) * (1.0 - lambda_init)

    nb = seq // QBLOCK
    qpos = jnp.arange(seq).reshape(nb, QBLOCK)
    out = lax.map(block, (to_query_blocks(q1), to_query_blocks(q2), qpos))
    return from_query_blocks(out)


def stick_breaking_attention(q, k, v):
    seq = q.shape[1]
    scale = HEAD_DIM ** -0.5
    kh, vh = [t.transpose(0, 2, 1, 3) for t in (k, v)]
    kpos = jnp.arange(seq)

    def block(args):
        qb, qpos = args
        z = jnp.einsum('bhqd,bhkd->bhqk', qb, kh).astype(jnp.float32) * scale
        strict = kpos[None, :] < qpos[:, None]
        log_keep = jnp.where(strict, jax.nn.log_sigmoid(-z), 0.0)
        later = lax.cumsum(log_keep, axis=3, reverse=True) - log_keep
        w = jnp.where(strict, jnp.exp(jax.nn.log_sigmoid(z) + later), 0.0)
        return jnp.einsum('bhqk,bhkd->bhqd', w.astype(vh.dtype), vh)

    nb = seq // QBLOCK
    qpos = jnp.arange(seq).reshape(nb, QBLOCK)
    out = lax.map(block, (to_query_blocks(q), qpos))
    return from_query_blocks(out)


def moe_ffn(xn, router_w, router_b, w_gate_up, b_gate_up, w_down, b_down):
    b, s, d = xn.shape
    t = xn.reshape(b * s, d)
    logits = (t @ router_w + router_b).astype(jnp.float32)
    top_v, top_i = lax.top_k(logits, TOP_K)
    gates = jax.nn.softmax(top_v, axis=-1)
    combine = jnp.einsum('nk,nke->ne', gates, jax.nn.one_hot(top_i, N_EXPERTS, dtype=jnp.float32))
    y = jnp.zeros((b * s, d), jnp.float32)
    for e in range(N_EXPERTS):
        gu = t @ w_gate_up[e] + b_gate_up[e]
        glu = jnp.minimum(gu[:, ::2], SWIGLU_LIMIT)
        lin = jnp.clip(gu[:, 1::2], -SWIGLU_LIMIT, SWIGLU_LIMIT)
        act = glu * jax.nn.sigmoid(SWIGLU_ALPHA * glu) * (lin + 1.0)
        y = y + combine[:, e:e + 1] * (act @ w_down[e] + b_down[e])
    return y.astype(xn.dtype).reshape(b, s, d)


def setup_inputs(seed: int = 0) -> dict:
    key = jax.random.key(seed)
    ks = jax.random.split(key, 22)
    f32 = jnp.float32
    nrm = lambda k, shape: jax.random.normal(k, shape, f32)
    return {
        "x": nrm(ks[0], (BATCH, SEQ, D_MODEL)),
        "p": nrm(ks[1], (DEPTH, BATCH, SEQ, PLE_DIM)),
        "w_in": nrm(ks[2], (DEPTH, D_MODEL, IN_WIDTH)) * D_MODEL ** -0.5,
        "w_out": nrm(ks[3], (DEPTH, MIX_WIDTH, D_MODEL)) * MIX_WIDTH ** -0.5,
        "attn_norm": 1.0 + 0.05 * nrm(ks[4], (DEPTH, D_MODEL)),
        "moe_norm": 1.0 + 0.05 * nrm(ks[5], (DEPTH, D_MODEL)),
        "rel_bias": 0.1 * nrm(ks[6], (NUM_BUCKETS, N_DIFF_HEADS)),
        "lambda_q1": 0.1 * nrm(ks[7], (DEPTH, HEAD_DIM)),
        "lambda_k1": 0.1 * nrm(ks[8], (DEPTH, HEAD_DIM)),
        "lambda_q2": 0.1 * nrm(ks[9], (DEPTH, HEAD_DIM)),
        "lambda_k2": 0.1 * nrm(ks[10], (DEPTH, HEAD_DIM)),
        "subln": 1.0 + 0.05 * nrm(ks[11], (DEPTH, DIFF_V_DIM)),
        "router_w": nrm(ks[12], (DEPTH, D_MODEL, N_EXPERTS)) * D_MODEL ** -0.5,
        "router_b": 0.01 * nrm(ks[13], (DEPTH, N_EXPERTS)),
        "w_gate_up": nrm(ks[14], (DEPTH, N_EXPERTS, D_MODEL, 2 * D_EXPERT)) * D_MODEL ** -0.5,
        "b_gate_up": 0.01 * nrm(ks[15], (DEPTH, N_EXPERTS, 2 * D_EXPERT)),
        "w_down": nrm(ks[16], (DEPTH, N_EXPERTS, D_EXPERT, D_MODEL)) * D_EXPERT ** -0.5,
        "b_down": 0.01 * nrm(ks[17], (DEPTH, N_EXPERTS, D_MODEL)),
        "ple_proj": nrm(ks[18], (DEPTH, PLE_DIM, D_MODEL)) * PLE_DIM ** -0.5,
        "ple_norm": 1.0 + 0.05 * nrm(ks[19], (DEPTH, D_MODEL)),
        "ple_gate": nrm(ks[20], (DEPTH, D_MODEL, D_MODEL)) * D_MODEL ** -0.5,
        "final_norm": 1.0 + 0.05 * nrm(ks[21], (D_MODEL,)),
    }


def reference(x, p, w_in, w_out, attn_norm, moe_norm, rel_bias, lambda_q1, lambda_k1,
              lambda_q2, lambda_k2, subln, router_w, router_b, w_gate_up, b_gate_up,
              w_down, b_down, ple_proj, ple_norm, ple_gate, final_norm):
    split_points = np.cumsum(SPLIT_SIZES)[:-1].tolist()
    h = x
    for i in range(DEPTH):
        hn = rms_norm(h, attn_norm[i])
        proj = hn @ w_in[i]
        b, s, _ = proj.shape
        dq1, dq2, dk1, dk2, dv, sq, sk, sv = jnp.split(proj, split_points, axis=-1)
        hd = lambda t, n: t.reshape(b, s, n, -1)
        lambda_init = 0.8 - 0.6 * math.exp(-0.3 * i)
        lam = (jnp.exp(jnp.sum(lambda_q1[i].astype(jnp.float32) * lambda_k1[i].astype(jnp.float32)))
               - jnp.exp(jnp.sum(lambda_q2[i].astype(jnp.float32) * lambda_k2[i].astype(jnp.float32)))
               + lambda_init)
        o_diff = diff_attention(hd(dq1, N_DIFF_HEADS), hd(dq2, N_DIFF_HEADS), hd(dk1, N_DIFF_HEADS),
                                hd(dk2, N_DIFF_HEADS), hd(dv, N_DIFF_HEADS), lam, subln[i],
                                lambda_init, rel_bias)
        o_sb = stick_breaking_attention(hd(sq, N_SB_HEADS), hd(sk, N_SB_HEADS), hd(sv, N_SB_HEADS))
        h = h + jnp.concatenate([o_diff, o_sb], axis=-1) @ w_out[i]
        h = h + moe_ffn(rms_norm(h, moe_norm[i]), router_w[i], router_b[i], w_gate_up[i],
                        b_gate_up[i], w_down[i], b_down[i])
        gate = jax.nn.sigmoid(h @ ple_gate[i])
        h = h + gate * rms_norm(p[i] @ ple_proj[i], ple_norm[i])
    return rms_norm(h, final_norm)
```

```python
import functools
import math

import numpy as np
import jax
import jax.numpy as jnp
from jax import lax
from jax.experimental import pallas as pl
from jax.experimental.pallas import tpu as pltpu

F32 = jnp.float32
BF16 = jnp.bfloat16
I32 = jnp.int32

HEAD_DIM = 64
DIFF_V_DIM = 2 * HEAD_DIM
NUM_BUCKETS = 32
MAX_DISTANCE = 128
TOP_K = 4
SWIGLU_LIMIT = 7.0
SWIGLU_ALPHA = 1.702
NORM_EPS = 1e-6
LOG2E = math.log2(math.e)

LANES = 128
ATTN_TILE = 256
TOKEN_TILE = 256
INPROJ_TILE = 512
EXPERT_TILE = 256
VMEM_LIMIT = 48 * 1024 * 1024


def _cparams(sem, vmem=None):
    return pltpu.CompilerParams(dimension_semantics=sem, vmem_limit_bytes=vmem)


def _inproj_kernel(x_ref, g_ref, w_ref, o_ref):
    x = x_ref[...]
    inv = lax.rsqrt(jnp.mean(x * x, axis=-1, keepdims=True) + NORM_EPS)
    hn = (x * inv * g_ref[...]).astype(BF16)
    o_ref[...] = jnp.dot(hn, w_ref[...], preferred_element_type=F32).astype(o_ref.dtype)


def _inproj(x2, g, w):
    n, d = x2.shape
    width = w.shape[1]
    tm = INPROJ_TILE
    return pl.pallas_call(
        _inproj_kernel,
        grid=(n // tm,),
        in_specs=[pl.BlockSpec((tm, d), lambda i: (i, 0)),
                  pl.BlockSpec((1, d), lambda i: (0, 0)),
                  pl.BlockSpec((d, width), lambda i: (0, 0))],
        out_specs=pl.BlockSpec((tm, width), lambda i: (i, 0)),
        out_shape=jax.ShapeDtypeStruct((n, width), BF16),
        compiler_params=_cparams(("arbitrary",), VMEM_LIMIT),
        name="inproj",
    )(x2, g, w)


_NT = (((1,), (1,)), ((), ()))


def _diff_attn_kernel(lq1_ref, lk1_ref, lq2_ref, lk2_ref, subln_ref, bias_ref, q_ref, k_ref, v_ref,
                      o_ref, s_scr, *, tq, lambda_init):
    qi = pl.program_id(2)
    q = q_ref[...]
    lane = lax.broadcasted_iota(I32, q.shape, 1)
    zero = jnp.zeros_like(q)
    qm = (jnp.where(lane < HEAD_DIM, q, zero), jnp.where(lane >= HEAD_DIM, q, zero))
    nk = qi + 1

    def scores(j, mx):
        kj = k_ref[pl.ds(pl.multiple_of(j * tq, tq), tq), :]
        b = bias_ref[0, jnp.minimum(qi - j, 2)]
        out = []
        for mp in range(2):
            s = lax.dot_general(qm[mp], kj, _NT, preferred_element_type=F32) + b
            s_scr[mp, j] = s
            out.append(jnp.maximum(mx[mp], jnp.maximum(s[:, :LANES], s[:, LANES:])))
        return tuple(out)

    ninf = jnp.full((tq, LANES), -jnp.inf, F32)
    mx = lax.fori_loop(0, nk, scores, (ninf, ninf))
    m = [jnp.max(mx[mp], axis=1, keepdims=True) for mp in range(2)]

    def weighted(j, carry):
        vj = v_ref[pl.ds(pl.multiple_of(j * tq, tq), tq), :]
        new = []
        for mp in range(2):
            l_acc, acc = carry[mp]
            p = jnp.exp2(s_scr[mp, j] - m[mp])
            l_acc = l_acc + (p[:, :LANES] + p[:, LANES:])
            acc = acc + jnp.dot(p.astype(BF16), vj, preferred_element_type=F32)
            new.append((l_acc, acc))
        return tuple(new)

    z = jnp.zeros((tq, LANES), F32)
    (l1, a1), (l2, a2) = lax.fori_loop(0, nk, weighted, ((z, z), (z, z)))
    l1 = jnp.sum(l1, axis=1, keepdims=True)
    l2 = jnp.sum(l2, axis=1, keepdims=True)
    lam = (jnp.exp(jnp.sum(lq1_ref[...] * lk1_ref[...], axis=1, keepdims=True))
           - jnp.exp(jnp.sum(lq2_ref[...] * lk2_ref[...], axis=1, keepdims=True)) + lambda_init)
    o = a1 / l1 - lam * (a2 / l2)
    o = o * lax.rsqrt(jnp.mean(o * o, axis=-1, keepdims=True) + NORM_EPS) * subln_ref[...]
    o_ref[...] = (o * (1.0 - lambda_init)).astype(o_ref.dtype)


def _diff_attention(proj, bias_tiles, lq1, lk1, lq2, lk2, subln, *, batch, seq, n_heads, lambda_init):
    n = proj.shape[0]
    tq = ATTN_TILE
    nq = seq // tq
    small = lambda shape: pl.BlockSpec(shape, lambda b, h, i: (0,) * len(shape))
    kern = functools.partial(_diff_attn_kernel, tq=tq, lambda_init=lambda_init)
    return pl.pallas_call(
        kern,
        grid=(batch, n_heads, nq),
        in_specs=[small((1, HEAD_DIM)), small((1, HEAD_DIM)), small((1, HEAD_DIM)), small((1, HEAD_DIM)),
                  small((1, DIFF_V_DIM)),
                  pl.BlockSpec((1, 3, tq, tq), lambda b, h, i: (h, 0, 0, 0)),
                  pl.BlockSpec((tq, LANES), lambda b, h, i: (b * nq + i, h)),
                  pl.BlockSpec((seq, LANES), lambda b, h, i: (b, n_heads + h)),
                  pl.BlockSpec((seq, LANES), lambda b, h, i: (b, 2 * n_heads + h))],
        out_specs=pl.BlockSpec((tq, LANES), lambda b, h, i: (b * nq + i, h)),
        out_shape=jax.ShapeDtypeStruct((n, n_heads * DIFF_V_DIM), BF16),
        scratch_shapes=[pltpu.VMEM((2, nq, tq, tq), F32)],
        compiler_params=_cparams(("arbitrary", "arbitrary", "arbitrary"), VMEM_LIMIT),
        name="diff_attention",
    )(lq1, lk1, lq2, lk2, subln, bias_tiles, proj, proj, proj)


def _sb_attn_kernel(u_ref, q_ref, k_ref, v_ref, o_ref, *, tq):
    qi = pl.program_id(2)
    q = q_ref[...]
    lane = lax.broadcasted_iota(I32, q.shape, 1)
    zero = jnp.zeros_like(q)
    row = lax.broadcasted_iota(I32, (tq, tq), 0)
    col = lax.broadcasted_iota(I32, (tq, tq), 1)
    strict = col < row
    u = u_ref[...]

    def tile(qm, j, carry, acc, diag):
        start = pl.multiple_of(j * tq, tq)
        kj = k_ref[pl.ds(start, tq), :]
        vj = v_ref[pl.ds(start, tq), :]
        n = lax.dot_general(qm, kj, _NT, preferred_element_type=F32)
        lk = jnp.minimum(n, 0.0) - jnp.log2(1.0 + jnp.exp2(-jnp.abs(n)))
        if diag:
            lk = jnp.where(strict, lk, 0.0)
        hi = lk.astype(BF16)
        lo = (lk - hi.astype(F32)).astype(BF16)
        sfx = jnp.dot(hi, u, preferred_element_type=F32) + jnp.dot(lo, u, preferred_element_type=F32)
        w = jnp.exp2(lk - n + (sfx + carry))
        if diag:
            w = jnp.where(strict, w, 0.0)
        acc = acc + jnp.dot(w.astype(BF16), vj, preferred_element_type=F32)
        carry = carry + (sfx[:, :1] + lk[:, :1])
        return carry, acc

    outs = []
    for hh in range(2):
        qm = jnp.where((lane < HEAD_DIM) if hh == 0 else (lane >= HEAD_DIM), q, zero)
        carry, acc = tile(qm, qi, jnp.zeros((tq, 1), F32), jnp.zeros((tq, LANES), F32), True)

        def body(jj, c, qm=qm):
            return tile(qm, qi - 1 - jj, c[0], c[1], False)

        carry, acc = lax.fori_loop(0, qi, body, (carry, acc))
        outs.append(acc)
    o_ref[...] = jnp.where(lane < HEAD_DIM, outs[0], outs[1]).astype(o_ref.dtype)


def _sb_attention(proj, u, *, batch, seq, n_pairs, col0):
    n = proj.shape[0]
    tq = ATTN_TILE
    nq = seq // tq
    kern = functools.partial(_sb_attn_kernel, tq=tq)
    return pl.pallas_call(
        kern,
        grid=(batch, n_pairs, nq),
        in_specs=[pl.BlockSpec((tq, tq), lambda b, p, i: (0, 0)),
                  pl.BlockSpec((tq, LANES), lambda b, p, i: (b * nq + i, col0 + p)),
                  pl.BlockSpec((seq, LANES), lambda b, p, i: (b, col0 + n_pairs + p)),
                  pl.BlockSpec((seq, LANES), lambda b, p, i: (b, col0 + 2 * n_pairs + p))],
        out_specs=pl.BlockSpec((tq, LANES), lambda b, p, i: (b * nq + i, p)),
        out_shape=jax.ShapeDtypeStruct((n, n_pairs * LANES), BF16),
        compiler_params=_cparams(("arbitrary", "arbitrary", "arbitrary"), VMEM_LIMIT),
        name="sb_attention",
    )(u, proj, proj, proj)


def _outproj_router_kernel(od_ref, os_ref, x_ref, wod_ref, wos_ref, g_ref, rwh_ref, rwl_ref, rb_ref, ut_ref,
                           h_ref, xn_ref, ti_ref, gt_ref, rk_ref, cnt_ref, cnt_scr, *, n_experts):
    @pl.when(pl.program_id(0) == 0)
    def _():
        cnt_scr[...] = jnp.zeros_like(cnt_scr)

    attn = (jnp.dot(od_ref[...], wod_ref[...], preferred_element_type=F32)
            + jnp.dot(os_ref[...], wos_ref[...], preferred_element_type=F32))
    h = x_ref[...] + attn
    h_ref[...] = h
    xn = h * lax.rsqrt(jnp.mean(h * h, axis=-1, keepdims=True) + NORM_EPS) * g_ref[...]
    xn_ref[...] = xn
    xh = xn.astype(BF16)
    xl = (xn - xh.astype(F32)).astype(BF16)
    rwh = rwh_ref[...]
    logits = (lax.dot_general(rwh, xh, _NT, preferred_element_type=F32)
              + lax.dot_general(rwh, xl, _NT, preferred_element_type=F32)
              + lax.dot_general(rwl_ref[...], xh, _NT, preferred_element_type=F32)
              + rb_ref[...])
    eidx = lax.broadcasted_iota(I32, logits.shape, 0)
    work = logits
    vals, idxs, sels = [], [], []
    for _ in range(TOP_K):
        mx = jnp.max(work, axis=0, keepdims=True)
        ix = jnp.min(jnp.where(work == mx, eidx, n_experts), axis=0, keepdims=True)
        sel = eidx == ix
        work = jnp.where(sel, -jnp.inf, work)
        vals.append(mx)
        idxs.append(ix)
        sels.append(sel)
    ex = [jnp.exp(v - vals[0]) for v in vals]
    den = ex[0] + ex[1] + ex[2] + ex[3]
    ti_ref[...] = jnp.concatenate(idxs, axis=0)
    gt_ref[...] = jnp.concatenate([e / den for e in ex], axis=0)
    onehot = jnp.where(sels[0] | sels[1] | sels[2] | sels[3], 1.0, 0.0)
    rank = jnp.dot(onehot.astype(BF16), ut_ref[...], preferred_element_type=F32) + cnt_scr[...]
    rk_ref[...] = jnp.concatenate(
        [jnp.sum(jnp.where(s, rank, 0.0), axis=0, keepdims=True) for s in sels], axis=0).astype(I32)
    cnt_scr[...] += jnp.sum(onehot, axis=1, keepdims=True)
    cnt_ref[...] = cnt_scr[...].astype(I32)


def _outproj_router(o_diff, o_sb, x2, wod, wos, g, rwh, rwl, rb, ut):
    n, d = x2.shape
    e = rwh.shape[0]
    tm = TOKEN_TILE
    row = lambda w: pl.BlockSpec((tm, w), lambda i: (i, 0))
    full = lambda a: pl.BlockSpec(a.shape, lambda i: (0,) * a.ndim)
    kcol = pl.BlockSpec((TOP_K, tm), lambda i: (0, i))
    kern = functools.partial(_outproj_router_kernel, n_experts=e)
    return pl.pallas_call(
        kern,
        grid=(n // tm,),
        in_specs=[row(o_diff.shape[1]), row(o_sb.shape[1]), row(d), full(wod), full(wos), full(g),
                  full(rwh), full(rwl), full(rb), full(ut)],
        out_specs=[row(d), row(d), kcol, kcol, kcol, pl.BlockSpec((e, 1), lambda i: (0, 0))],
        out_shape=[jax.ShapeDtypeStruct((n, d), F32), jax.ShapeDtypeStruct((n, d), F32),
                   jax.ShapeDtypeStruct((TOP_K, n), I32), jax.ShapeDtypeStruct((TOP_K, n), F32),
                   jax.ShapeDtypeStruct((TOP_K, n), I32), jax.ShapeDtypeStruct((e, 1), I32)],
        scratch_shapes=[pltpu.VMEM((e, 1), F32)],
        compiler_params=_cparams(("arbitrary",), VMEM_LIMIT),
        name="outproj_router",
    )(o_diff, o_sb, x2, wod, wos, g, rwh, rwl, rb, ut)


def _row_copy(src_ref, src_row, dst_ref, dst_row, sem):
    return pltpu.make_async_copy(src_ref.at[pl.ds(src_row, 1), :], dst_ref.at[pl.ds(dst_row, 1), :], sem)


def _dispatch_kernel(fill_lo_ref, fill_hi_ref, nvalid_ref, pos_ref, xn_ref, xs_ref, zero_scr, sems,
                     *, tm, te, n_experts, n_tiles):
    sem, tail_sem, fill_sem = sems.at[0], sems.at[1], sems.at[2]

    def issue(r, c):
        for k in range(TOP_K):
            _row_copy(xn_ref, r, xs_ref, pos_ref[0, 0, k * tm + r], sem).start()
        return c

    lax.fori_loop(0, tm, issue, 0)

    @pl.when(pl.program_id(0) == 0)
    def _():
        zero_scr[...] = jnp.zeros_like(zero_scr)

        def tail_copy(t):
            return pltpu.make_async_copy(zero_scr, xs_ref.at[pl.ds(pl.multiple_of(t * te, te), te), :],
                                         tail_sem)

        def tail(t, c):
            tail_copy(t).start()
            return c

        lax.fori_loop(nvalid_ref[0], n_tiles, tail, 0)

        def drain_tail(t, c):
            tail_copy(t).wait()
            return c

        lax.fori_loop(nvalid_ref[0], n_tiles, drain_tail, 0)
        for e in range(n_experts):
            def fill(r, c):
                _row_copy(zero_scr, 0, xs_ref, r, fill_sem).start()
                return c
            lax.fori_loop(fill_lo_ref[e], fill_hi_ref[e], fill, 0)
        for e in range(n_experts):
            def drain_fill(r, c):
                _row_copy(zero_scr, 0, xs_ref, r, fill_sem).wait()
                return c
            lax.fori_loop(fill_lo_ref[e], fill_hi_ref[e], drain_fill, 0)

    def drain(r, c):
        for k in range(TOP_K):
            _row_copy(xn_ref, r, xs_ref, pos_ref[0, 0, k * tm + r], sem).wait()
        return c

    lax.fori_loop(0, tm, drain, 0)


def _dispatch(xn, pos_tiles, fill_lo, fill_hi, n_valid, n_rows):
    n, d = xn.shape
    tm = TOKEN_TILE
    te = EXPERT_TILE
    e = fill_lo.shape[0]
    kern = functools.partial(_dispatch_kernel, tm=tm, te=te, n_experts=e, n_tiles=n_rows // te)
    grid_spec = pltpu.PrefetchScalarGridSpec(
        num_scalar_prefetch=3,
        grid=(n // tm,),
        in_specs=[pl.BlockSpec((1, 1, TOP_K * tm), lambda i, lo, hi, nv: (i, 0, 0), memory_space=pltpu.SMEM),
                  pl.BlockSpec((tm, d), lambda i, lo, hi, nv: (i, 0))],
        out_specs=pl.BlockSpec(memory_space=pl.ANY),
        scratch_shapes=[pltpu.VMEM((te, d), F32), pltpu.SemaphoreType.DMA((3,))],
    )
    return pl.pallas_call(
        kern,
        grid_spec=grid_spec,
        out_shape=jax.ShapeDtypeStruct((n_rows, d), F32),
        compiler_params=_cparams(("arbitrary",)),
        name="dispatch",
    )(fill_lo, fill_hi, n_valid, pos_tiles, xn)


def _expert_kernel(blk_ref, exp_ref, valid_ref, xs_ref, wg_ref, wl_ref, bg_ref, bl_ref, wd_ref, bd_ref, o_ref):
    valid = valid_ref[pl.program_id(0)] == 1

    @pl.when(jnp.logical_not(valid))
    def _():
        o_ref[...] = jnp.zeros_like(o_ref)

    @pl.when(valid)
    def _():
        x = xs_ref[...].astype(BF16)
        glu = jnp.dot(x, wg_ref[0], preferred_element_type=F32) + bg_ref[0]
        lin = jnp.dot(x, wl_ref[0], preferred_element_type=F32) + bl_ref[0]
        glu = jnp.minimum(glu, SWIGLU_LIMIT)
        lin = jnp.clip(lin, -SWIGLU_LIMIT, SWIGLU_LIMIT)
        act = glu * jax.nn.sigmoid(SWIGLU_ALPHA * glu) * (lin + 1.0)
        o_ref[...] = jnp.dot(act.astype(BF16), wd_ref[0], preferred_element_type=F32) + bd_ref[0]


def _experts(xs, tile_blk, tile_exp, tile_valid, wg, wl, bg, bl, wd, bd):
    p, d = xs.shape
    de = wg.shape[2]
    tm = EXPERT_TILE
    grid_spec = pltpu.PrefetchScalarGridSpec(
        num_scalar_prefetch=3,
        grid=(p // tm,),
        in_specs=[pl.BlockSpec((tm, d), lambda i, blk, ex, va: (blk[i], 0)),
                  pl.BlockSpec((1, d, de), lambda i, blk, ex, va: (ex[i], 0, 0)),
                  pl.BlockSpec((1, d, de), lambda i, blk, ex, va: (ex[i], 0, 0)),
                  pl.BlockSpec((1, 1, de), lambda i, blk, ex, va: (ex[i], 0, 0)),
                  pl.BlockSpec((1, 1, de), lambda i, blk, ex, va: (ex[i], 0, 0)),
                  pl.BlockSpec((1, de, d), lambda i, blk, ex, va: (ex[i], 0, 0)),
                  pl.BlockSpec((1, 1, d), lambda i, blk, ex, va: (ex[i], 0, 0))],
        out_specs=pl.BlockSpec((tm, d), lambda i, blk, ex, va: (i, 0)),
    )
    return pl.pallas_call(
        _expert_kernel,
        grid_spec=grid_spec,
        out_shape=jax.ShapeDtypeStruct((p, d), F32),
        compiler_params=_cparams(("arbitrary",), VMEM_LIMIT),
        name="experts",
    )(tile_blk, tile_exp, tile_valid, xs, wg, wl, bg, bl, wd, bd)


def _combine_kernel(pos_ref, h_ref, gt_ref, p_ref, wp_ref, gp_ref, wg_ref, gf_ref, ys_ref, o_ref,
                    rows_scr, sem, *, tm):
    def issue(r, c):
        for k in range(TOP_K):
            pltpu.make_async_copy(ys_ref.at[pl.ds(pos_ref[0, 0, k * tm + r], 1), :],
                                  rows_scr.at[k, pl.ds(r, 1), :], sem).start()
        return c

    lax.fori_loop(0, tm, issue, 0)

    emb = jnp.dot(p_ref[...].astype(BF16), wp_ref[...], preferred_element_type=F32)
    emb = emb * lax.rsqrt(jnp.mean(emb * emb, axis=-1, keepdims=True) + NORM_EPS) * gp_ref[...]

    def drain(r, c):
        for k in range(TOP_K):
            pltpu.make_async_copy(ys_ref.at[pl.ds(pos_ref[0, 0, k * tm + r], 1), :],
                                  rows_scr.at[k, pl.ds(r, 1), :], sem).wait()
        return c

    lax.fori_loop(0, tm, drain, 0)

    gates = gt_ref[...]
    y = gates[:, 0:1] * rows_scr[0]
    for k in range(1, TOP_K):
        y = y + gates[:, k:k + 1] * rows_scr[k]
    h = h_ref[...] + y
    gate = jax.nn.sigmoid(jnp.dot(h.astype(BF16), wg_ref[...], preferred_element_type=F32))
    h = h + gate * emb
    o_ref[...] = h * lax.rsqrt(jnp.mean(h * h, axis=-1, keepdims=True) + NORM_EPS) * gf_ref[...]


def _combine(pos_tiles, h1, gates_t, p2, wp, gp, wg, gf, ys):
    n, d = h1.shape
    tm = TOKEN_TILE
    row = lambda w: pl.BlockSpec((tm, w), lambda i: (i, 0))
    full = lambda a: pl.BlockSpec(a.shape, lambda i: (0,) * a.ndim)
    kern = functools.partial(_combine_kernel, tm=tm)
    return pl.pallas_call(
        kern,
        grid=(n // tm,),
        in_specs=[pl.BlockSpec((1, 1, TOP_K * tm), lambda i: (i, 0, 0), memory_space=pltpu.SMEM),
                  row(d), row(TOP_K), row(p2.shape[1]), full(wp), full(gp), full(wg), full(gf),
                  pl.BlockSpec(memory_space=pl.ANY)],
        out_specs=row(d),
        out_shape=jax.ShapeDtypeStruct((n, d), F32),
        scratch_shapes=[pltpu.VMEM((TOP_K, tm, d), F32), pltpu.SemaphoreType.DMA(())],
        compiler_params=_cparams(("arbitrary",), VMEM_LIMIT),
        name="combine_ple",
    )(pos_tiles, h1, gates_t, p2, wp, gp, wg, gf, ys)


def _t5_bucket(rel):
    n = jnp.maximum(rel, 0)
    max_exact = NUM_BUCKETS // 2
    nf = jnp.maximum(n, 1).astype(F32)
    large = max_exact + (jnp.log(nf / max_exact) / math.log(MAX_DISTANCE / max_exact)
                         * (NUM_BUCKETS - max_exact)).astype(I32)
    large = jnp.minimum(large, NUM_BUCKETS - 1)
    return jnp.where(n < max_exact, n, large)


def _bias_tiles(rel_bias, tq):
    i = jnp.arange(tq)[:, None]
    j = jnp.arange(tq)[None, :]
    tiles = []
    for d in range(3):
        rel = d * tq + i - j
        b = rel_bias.astype(F32)[_t5_bucket(rel)] * LOG2E
        b = jnp.where((rel >= 0)[:, :, None], b, -jnp.inf)
        tiles.append(b.transpose(2, 0, 1))
    return jnp.stack(tiles, axis=1)


def _far_bias_is_constant(tq):
    n = np.arange(tq + 1, 4 * tq).astype(np.float32)
    max_exact = NUM_BUCKETS // 2
    large = max_exact + (np.log(n / max_exact) / math.log(MAX_DISTANCE / max_exact)
                         * (NUM_BUCKETS - max_exact)).astype(np.int32)
    return bool(np.all(np.minimum(large, NUM_BUCKETS - 1) == NUM_BUCKETS - 1))


def _tile_pos(pos, tm):
    k, n = pos.shape
    return pos.reshape(k, n // tm, tm).transpose(1, 0, 2).reshape(n // tm, 1, k * tm)


def kernel(x, p, w_in, w_out, attn_norm, moe_norm, rel_bias, lambda_q1, lambda_k1, lambda_q2, lambda_k2, subln,
           router_w, router_b, w_gate_up, b_gate_up, w_down, b_down, ple_proj, ple_norm, ple_gate, final_norm):
    batch, seq, d = x.shape
    depth = w_in.shape[0]
    assert depth == 1
    n = batch * seq
    n_experts = router_w.shape[2]
    n_diff = (d // 2) // (2 * HEAD_DIM)
    n_sb = (d - d // 2) // HEAD_DIM
    qk_w = n_diff * HEAD_DIM
    assert 2 * HEAD_DIM == LANES and n_sb % 2 == 0
    assert seq % ATTN_TILE == 0 and n % INPROJ_TILE == 0 and n % TOKEN_TILE == 0
    assert _far_bias_is_constant(ATTN_TILE)
    lambda_init = 0.8 - 0.6 * math.exp(-0.3 * 0)
    scale = HEAD_DIM ** -0.5

    x2 = x.reshape(n, d)
    w = w_in[0]
    dq1, dq2, dk1, dk2 = (w[:, i * qk_w:(i + 1) * qk_w].reshape(d, n_diff, HEAD_DIM) for i in range(4))
    o = 4 * qk_w
    dv = w[:, o:o + n_diff * DIFF_V_DIM]
    o += n_diff * DIFF_V_DIM
    sq, sk, sv = (w[:, o + i * n_sb * HEAD_DIM:o + (i + 1) * n_sb * HEAD_DIM] for i in range(3))
    wq = jnp.concatenate([dq1, dq2], axis=2).reshape(d, 2 * qk_w) * (scale * LOG2E)
    wk = jnp.concatenate([dk1, dk2], axis=2).reshape(d, 2 * qk_w)
    w_all = jnp.concatenate([wq, wk, dv, sq * (-scale * LOG2E), sk, sv], axis=1).astype(BF16)

    proj = _inproj(x2, attn_norm[0].reshape(1, d), w_all)

    o_diff = _diff_attention(
        proj, _bias_tiles(rel_bias, ATTN_TILE),
        lambda_q1[0].reshape(1, HEAD_DIM).astype(F32), lambda_k1[0].reshape(1, HEAD_DIM).astype(F32),
        lambda_q2[0].reshape(1, HEAD_DIM).astype(F32), lambda_k2[0].reshape(1, HEAD_DIM).astype(F32),
        subln[0].reshape(1, DIFF_V_DIM).astype(F32),
        batch=batch, seq=seq, n_heads=n_diff, lambda_init=lambda_init)

    t = ATTN_TILE
    u = (jnp.arange(t)[:, None] > jnp.arange(t)[None, :]).astype(BF16)
    o_sb = _sb_attention(proj, u, batch=batch, seq=seq, n_pairs=n_sb // 2, col0=3 * n_diff)

    wo = w_out[0].astype(BF16)
    rw = router_w[0].T
    rwh = rw.astype(BF16)
    rwl = (rw - rwh.astype(F32)).astype(BF16)
    tt = TOKEN_TILE
    ut = (jnp.arange(tt)[:, None] < jnp.arange(tt)[None, :]).astype(BF16)
    h1, xn, top_i, gates, rank, counts = _outproj_router(
        o_diff, o_sb, x2, wo[:n_diff * DIFF_V_DIM], wo[n_diff * DIFF_V_DIM:], moe_norm[0].reshape(1, d),
        rwh, rwl, router_b[0].reshape(n_experts, 1).astype(F32), ut)

    te = EXPERT_TILE
    n_rows = n * TOP_K + n_experts * te
    n_tiles = n_rows // te
    cnt = counts[:, 0]
    tiles_e = (cnt + te - 1) // te
    tile_end = jnp.cumsum(tiles_e)
    base = (tile_end - tiles_e) * te
    n_valid = tile_end[-1]
    tidx = jnp.arange(n_tiles, dtype=I32)
    tile_valid = (tidx < n_valid).astype(I32)
    tile_blk = jnp.minimum(tidx, n_valid - 1).astype(I32)
    tile_exp = jnp.minimum(jnp.searchsorted(tile_end, tile_blk, side="right"), n_experts - 1).astype(I32)
    pos = (base[top_i] + rank).astype(I32)
    pos_tiles = _tile_pos(pos, tt)
    fill_lo = (base + cnt).astype(I32)
    fill_hi = (tile_end * te).astype(I32)

    xs = _dispatch(xn, pos_tiles, fill_lo, fill_hi, n_valid.reshape(1).astype(I32), n_rows)

    wgu = w_gate_up[0]
    bgu = b_gate_up[0]
    ys = _experts(xs, tile_blk, tile_exp, tile_valid,
                  wgu[:, :, 0::2].astype(BF16), wgu[:, :, 1::2].astype(BF16),
                  bgu[:, None, 0::2].astype(F32), bgu[:, None, 1::2].astype(F32),
                  w_down[0].astype(BF16), b_down[0][:, None, :].astype(F32))

    out = _combine(pos_tiles, h1, gates.T, p[0].reshape(n, -1), ple_proj[0].astype(BF16),
                   ple_norm[0].reshape(1, d), ple_gate[0].astype(BF16), final_norm.reshape(1, d), ys)
    return out.reshape(batch, seq, d)
```

```python
import functools
import math

import numpy as np
import jax
import jax.numpy as jnp
from jax import lax
from jax.experimental import pallas as pl
from jax.experimental.pallas import tpu as pltpu

F32 = jnp.float32
BF16 = jnp.bfloat16
I32 = jnp.int32

HEAD_DIM = 64
DIFF_V_DIM = 2 * HEAD_DIM
NUM_BUCKETS = 32
MAX_DISTANCE = 128
TOP_K = 4
SWIGLU_LIMIT = 7.0
SWIGLU_ALPHA = 1.702
NORM_EPS = 1e-6
LOG2E = math.log2(math.e)

LANES = 128
MXU_DIM = 256
ATTN_TILE = 256
TOKEN_TILE = 256
INPROJ_TILE = 512
EXPERT_TILE = 256
VMEM_LIMIT = 48 * 1024 * 1024
EXPERT_VMEM_LIMIT = 56 * 1024 * 1024


def _cparams(sem, vmem=None):
    return pltpu.CompilerParams(dimension_semantics=sem, vmem_limit_bytes=vmem)


def _inproj_kernel(x_ref, g_ref, w_ref, o_ref):
    x = x_ref[...]
    inv = lax.rsqrt(jnp.mean(x * x, axis=-1, keepdims=True) + NORM_EPS)
    hn = (x * inv * g_ref[...]).astype(BF16)
    o_ref[...] = jnp.dot(hn, w_ref[...], preferred_element_type=F32).astype(o_ref.dtype)


def _inproj(x2, g, w):
    n, d = x2.shape
    width = w.shape[1]
    tm = INPROJ_TILE
    return pl.pallas_call(
        _inproj_kernel,
        grid=(n // tm,),
        in_specs=[pl.BlockSpec((tm, d), lambda i: (i, 0)),
                  pl.BlockSpec((1, d), lambda i: (0, 0)),
                  pl.BlockSpec((d, width), lambda i: (0, 0))],
        out_specs=pl.BlockSpec((tm, width), lambda i: (i, 0)),
        out_shape=jax.ShapeDtypeStruct((n, width), BF16),
        compiler_params=_cparams(("arbitrary",), VMEM_LIMIT),
        name="inproj",
    )(x2, g, w)


_NT = (((1,), (1,)), ((), ()))


def _diff_attn_kernel(lq1_ref, lk1_ref, lq2_ref, lk2_ref, subln_ref, bias_ref, q1_ref, q2_ref, k1_ref, k2_ref,
                      v_ref, o_ref, s_scr, *, tq, lambda_init):
    h = pl.program_id(1)
    qi = pl.program_id(2)
    lane = lax.broadcasted_iota(I32, (tq, LANES), 1)
    mine = (lane >= HEAD_DIM) == (h % 2 == 1)
    zq = jnp.zeros((tq, LANES), q1_ref.dtype)
    q1 = jnp.where(mine, q1_ref[...], zq)
    q2 = jnp.where(mine, q2_ref[...], zq)
    qq = jnp.concatenate([jnp.concatenate([q1, zq], axis=1), jnp.concatenate([zq, q2], axis=1)], axis=0)
    n_pairs = (qi + 2) // 2

    def key_tile(g, t):
        j = 2 * g + t
        d = qi - j
        return j, pl.multiple_of(j * tq, tq), jnp.where(d < 0, 3, jnp.minimum(d, 2))

    def scores(g, mx):
        for t in range(2):
            j, start, bi = key_tile(g, t)
            b = bias_ref[0, bi]
            kk = jnp.concatenate([k1_ref[pl.ds(start, tq), :], k2_ref[pl.ds(start, tq), :]], axis=1)
            s = lax.dot_general(qq, kk, _NT, preferred_element_type=F32) + jnp.concatenate([b, b], axis=0)
            s_scr[j] = s
            mx = jnp.maximum(mx, jnp.maximum(s[:, :LANES], s[:, LANES:]))
        return mx

    mx = lax.fori_loop(0, n_pairs, scores, jnp.full((2 * tq, LANES), -jnp.inf, F32))
    m = jnp.max(mx, axis=1, keepdims=True)

    def weighted(g, carry):
        l_acc, acc = carry
        for t in range(2):
            j, start, _ = key_tile(g, t)
            p = jnp.exp2(s_scr[j] - m)
            l_acc = l_acc + (p[:, :LANES] + p[:, LANES:])
            acc = acc + jnp.dot(p.astype(BF16), v_ref[pl.ds(start, tq), :], preferred_element_type=F32)
        return l_acc, acc

    z = jnp.zeros((2 * tq, LANES), F32)
    l_acc, acc = lax.fori_loop(0, n_pairs, weighted, (z, z))
    l = jnp.sum(l_acc, axis=1, keepdims=True)
    l1, l2 = l[:tq], l[tq:]
    a1, a2 = acc[:tq], acc[tq:]
    lam = (jnp.exp(jnp.sum(lq1_ref[...] * lk1_ref[...], axis=1, keepdims=True))
           - jnp.exp(jnp.sum(lq2_ref[...] * lk2_ref[...], axis=1, keepdims=True)) + lambda_init)
    o = a1 / l1 - lam * (a2 / l2)
    o = o * lax.rsqrt(jnp.mean(o * o, axis=-1, keepdims=True) + NORM_EPS) * subln_ref[...]
    o_ref[...] = (o * (1.0 - lambda_init)).astype(o_ref.dtype)


def _diff_attention(proj, bias_tiles, lq1, lk1, lq2, lk2, subln, *, batch, seq, n_heads, lambda_init):
    n = proj.shape[0]
    tq = ATTN_TILE
    nq = seq // tq
    assert nq % 2 == 0
    nb = n_heads // 2
    small = lambda shape: pl.BlockSpec(shape, lambda b, h, i: (0,) * len(shape))
    kern = functools.partial(_diff_attn_kernel, tq=tq, lambda_init=lambda_init)
    qspec = lambda grp: pl.BlockSpec((tq, LANES), lambda b, h, i: (b * nq + i, grp * nb + h // 2))
    kspec = lambda grp: pl.BlockSpec((seq, LANES), lambda b, h, i: (b, grp * nb + h // 2))
    return pl.pallas_call(
        kern,
        grid=(batch, n_heads, nq),
        in_specs=[small((1, HEAD_DIM)), small((1, HEAD_DIM)), small((1, HEAD_DIM)), small((1, HEAD_DIM)),
                  small((1, DIFF_V_DIM)),
                  pl.BlockSpec((1, 4, tq, tq), lambda b, h, i: (h, 0, 0, 0)),
                  qspec(0), qspec(1), kspec(2), kspec(3),
                  pl.BlockSpec((seq, LANES), lambda b, h, i: (b, 4 * nb + h))],
        out_specs=pl.BlockSpec((tq, LANES), lambda b, h, i: (b * nq + i, h)),
        out_shape=jax.ShapeDtypeStruct((n, n_heads * DIFF_V_DIM), BF16),
        scratch_shapes=[pltpu.VMEM((nq, 2 * tq, tq), F32)],
        compiler_params=_cparams(("arbitrary", "arbitrary", "arbitrary"), VMEM_LIMIT),
        name="diff_attention",
    )(lq1, lk1, lq2, lk2, subln, bias_tiles, proj, proj, proj, proj, proj)


def _sb_attn_kernel(u_ref, q_ref, k_ref, v_ref, o_ref, *, tq):
    qi = pl.program_id(2)
    q = q_ref[...]
    lane = lax.broadcasted_iota(I32, q.shape, 1)
    zero = jnp.zeros_like(q)
    q2 = jnp.concatenate([jnp.where(lane < HEAD_DIM, q, zero), jnp.where(lane >= HEAD_DIM, q, zero)], axis=0)
    row = lax.broadcasted_iota(I32, (2 * tq, tq), 0)
    col = lax.broadcasted_iota(I32, (2 * tq, tq), 1)
    strict = col < jnp.where(row >= tq, row - tq, row)
    u = u_ref[...]

    def tile(j, carry, acc, diag, valid):
        start = pl.multiple_of(j * tq, tq)
        kj = k_ref[pl.ds(start, tq), :]
        vj = v_ref[pl.ds(start, tq), :]
        n = lax.dot_general(q2, kj, _NT, preferred_element_type=F32)
        lk = jnp.minimum(n, 0.0) - jnp.log2(1.0 + jnp.exp2(-jnp.abs(n)))
        if diag:
            lk = jnp.where(strict, lk, 0.0)
        hi = lk.astype(BF16)
        lo = (lk - hi.astype(F32)).astype(BF16)
        sf = jnp.dot(jnp.concatenate([hi, lo], axis=0), u, preferred_element_type=F32)
        sfx = sf[:2 * tq] + sf[2 * tq:]
        w = jnp.exp2(lk - n + (sfx + carry))
        if diag:
            w = jnp.where(strict, w, 0.0)
        if valid is not None:
            w = jnp.where(valid, w, 0.0)
        acc = acc + jnp.dot(w.astype(BF16), vj, preferred_element_type=F32)
        carry = carry + (sfx[:, :1] + lk[:, :1])
        return carry, acc

    def group(g, state, diag):
        ja = qi - 2 * g
        jb = ja - 1
        carry, acc = tile(ja, state[0], state[1], diag, None)
        return tile(jnp.maximum(jb, 0), carry, acc, False, jb >= 0)

    state = group(0, (jnp.zeros((2 * tq, 1), F32), jnp.zeros((2 * tq, LANES), F32)), True)
    acc = lax.fori_loop(1, (qi + 2) // 2, lambda g, st: group(g, st, False), state)[1]
    o_ref[...] = jnp.where(lane < HEAD_DIM, acc[:tq], acc[tq:]).astype(o_ref.dtype)


def _sb_attention(proj, u, *, batch, seq, n_pairs, col0):
    n = proj.shape[0]
    tq = ATTN_TILE
    nq = seq // tq
    kern = functools.partial(_sb_attn_kernel, tq=tq)
    return pl.pallas_call(
        kern,
        grid=(batch, n_pairs, nq),
        in_specs=[pl.BlockSpec((tq, tq), lambda b, p, i: (0, 0)),
                  pl.BlockSpec((tq, LANES), lambda b, p, i: (b * nq + i, col0 + p)),
                  pl.BlockSpec((seq, LANES), lambda b, p, i: (b, col0 + n_pairs + p)),
                  pl.BlockSpec((seq, LANES), lambda b, p, i: (b, col0 + 2 * n_pairs + p))],
        out_specs=pl.BlockSpec((tq, LANES), lambda b, p, i: (b * nq + i, p)),
        out_shape=jax.ShapeDtypeStruct((n, n_pairs * LANES), BF16),
        compiler_params=_cparams(("arbitrary", "arbitrary", "arbitrary"), VMEM_LIMIT),
        name="sb_attention",
    )(u, proj, proj, proj)


def _outproj_router_kernel(od_ref, os_ref, x_ref, wod_ref, wos_ref, g_ref, rwh_ref, rwl_ref, rb_ref, ut_ref,
                           h_ref, xn_ref, ti_ref, gt_ref, rk_ref, cnt_ref, cnt_scr, *, n_experts):
    @pl.when(pl.program_id(0) == 0)
    def _():
        cnt_scr[...] = jnp.zeros_like(cnt_scr)

    attn = (jnp.dot(od_ref[...], wod_ref[...], preferred_element_type=F32)
            + jnp.dot(os_ref[...], wos_ref[...], preferred_element_type=F32))
    h = x_ref[...] + attn
    h_ref[...] = h
    xn = h * lax.rsqrt(jnp.mean(h * h, axis=-1, keepdims=True) + NORM_EPS) * g_ref[...]
    xn_ref[...] = xn
    xh = xn.astype(BF16)
    xl = (xn - xh.astype(F32)).astype(BF16)
    rwh = rwh_ref[...]
    logits = (lax.dot_general(rwh, xh, _NT, preferred_element_type=F32)
              + lax.dot_general(rwh, xl, _NT, preferred_element_type=F32)
              + lax.dot_general(rwl_ref[...], xh, _NT, preferred_element_type=F32)
              + rb_ref[...])
    eidx = lax.broadcasted_iota(I32, logits.shape, 0)
    work = logits
    vals, idxs, sels = [], [], []
    for _ in range(TOP_K):
        mx = jnp.max(work, axis=0, keepdims=True)
        ix = jnp.min(jnp.where(work == mx, eidx, n_experts), axis=0, keepdims=True)
        sel = eidx == ix
        work = jnp.where(sel, -jnp.inf, work)
        vals.append(mx)
        idxs.append(ix)
        sels.append(sel)
    ex = [jnp.exp(v - vals[0]) for v in vals]
    den = ex[0] + ex[1] + ex[2] + ex[3]
    ti_ref[...] = jnp.concatenate(idxs, axis=0)
    gt_ref[...] = jnp.concatenate([e / den for e in ex], axis=0)
    onehot = jnp.where(sels[0] | sels[1] | sels[2] | sels[3], 1.0, 0.0)
    rank = jnp.dot(onehot.astype(BF16), ut_ref[...], preferred_element_type=F32) + cnt_scr[...]
    rk_ref[...] = jnp.concatenate(
        [jnp.sum(jnp.where(s, rank, 0.0), axis=0, keepdims=True) for s in sels], axis=0).astype(I32)
    cnt_scr[...] += jnp.sum(onehot, axis=1, keepdims=True)
    cnt_ref[...] = cnt_scr[...].astype(I32)


def _outproj_router(o_diff, o_sb, x2, wod, wos, g, rwh, rwl, rb, ut):
    n, d = x2.shape
    e = rwh.shape[0]
    tm = TOKEN_TILE
    row = lambda w: pl.BlockSpec((tm, w), lambda i: (i, 0))
    full = lambda a: pl.BlockSpec(a.shape, lambda i: (0,) * a.ndim)
    kcol = pl.BlockSpec((TOP_K, tm), lambda i: (0, i))
    kern = functools.partial(_outproj_router_kernel, n_experts=e)
    return pl.pallas_call(
        kern,
        grid=(n // tm,),
        in_specs=[row(o_diff.shape[1]), row(o_sb.shape[1]), row(d), full(wod), full(wos), full(g),
                  full(rwh), full(rwl), full(rb), full(ut)],
        out_specs=[row(d), row(d), kcol, kcol, kcol, pl.BlockSpec((e, 1), lambda i: (0, 0))],
        out_shape=[jax.ShapeDtypeStruct((n, d), F32), jax.ShapeDtypeStruct((n, d), F32),
                   jax.ShapeDtypeStruct((TOP_K, n), I32), jax.ShapeDtypeStruct((TOP_K, n), F32),
                   jax.ShapeDtypeStruct((TOP_K, n), I32), jax.ShapeDtypeStruct((e, 1), I32)],
        scratch_shapes=[pltpu.VMEM((e, 1), F32)],
        compiler_params=_cparams(("arbitrary",), VMEM_LIMIT),
        name="outproj_router",
    )(o_diff, o_sb, x2, wod, wos, g, rwh, rwl, rb, ut)


def _row_copy(src_ref, src_row, dst_ref, dst_row, sem):
    return pltpu.make_async_copy(src_ref.at[pl.ds(src_row, 1), :], dst_ref.at[pl.ds(dst_row, 1), :], sem)


def _dispatch_kernel(fill_lo_ref, fill_hi_ref, nvalid_ref, pos_ref, xn_ref, xs_ref, zero_scr, sems,
                     *, tm, te, n_experts, n_tiles):
    sem, tail_sem, fill_sem = sems.at[0], sems.at[1], sems.at[2]

    def issue(r, c):
        for k in range(TOP_K):
            _row_copy(xn_ref, r, xs_ref, pos_ref[0, 0, k * tm + r], sem).start()
        return c

    lax.fori_loop(0, tm, issue, 0)

    @pl.when(pl.program_id(0) == 0)
    def _():
        zero_scr[...] = jnp.zeros_like(zero_scr)

        def tail_copy(t):
            return pltpu.make_async_copy(zero_scr, xs_ref.at[pl.ds(pl.multiple_of(t * te, te), te), :],
                                         tail_sem)

        def tail(t, c):
            tail_copy(t).start()
            return c

        lax.fori_loop(nvalid_ref[0], n_tiles, tail, 0)

        def drain_tail(t, c):
            tail_copy(t).wait()
            return c

        lax.fori_loop(nvalid_ref[0], n_tiles, drain_tail, 0)
        for e in range(n_experts):
            def fill(r, c):
                _row_copy(zero_scr, 0, xs_ref, r, fill_sem).start()
                return c
            lax.fori_loop(fill_lo_ref[e], fill_hi_ref[e], fill, 0)
        for e in range(n_experts):
            def drain_fill(r, c):
                _row_copy(zero_scr, 0, xs_ref, r, fill_sem).wait()
                return c
            lax.fori_loop(fill_lo_ref[e], fill_hi_ref[e], drain_fill, 0)

    def drain(r, c):
        for k in range(TOP_K):
            _row_copy(xn_ref, r, xs_ref, pos_ref[0, 0, k * tm + r], sem).wait()
        return c

    lax.fori_loop(0, tm, drain, 0)


def _dispatch(xn, pos_tiles, fill_lo, fill_hi, n_valid, n_rows):
    n, d = xn.shape
    tm = TOKEN_TILE
    te = EXPERT_TILE
    e = fill_lo.shape[0]
    kern = functools.partial(_dispatch_kernel, tm=tm, te=te, n_experts=e, n_tiles=n_rows // te)
    grid_spec = pltpu.PrefetchScalarGridSpec(
        num_scalar_prefetch=3,
        grid=(n // tm,),
        in_specs=[pl.BlockSpec((1, 1, TOP_K * tm), lambda i, lo, hi, nv: (i, 0, 0), memory_space=pltpu.SMEM),
                  pl.BlockSpec((tm, d), lambda i, lo, hi, nv: (i, 0))],
        out_specs=pl.BlockSpec(memory_space=pl.ANY),
        scratch_shapes=[pltpu.VMEM((te, d), F32), pltpu.SemaphoreType.DMA((3,))],
    )
    return pl.pallas_call(
        kern,
        grid_spec=grid_spec,
        out_shape=jax.ShapeDtypeStruct((n_rows, d), F32),
        compiler_params=_cparams(("arbitrary",)),
        name="dispatch",
    )(fill_lo, fill_hi, n_valid, pos_tiles, xn)


def _expert_kernel(blk_ref, exp_ref, valid_ref, first_ref, xs_ref, wgu_ref, bg_ref, bl_ref, wd_ref, bd_ref,
                   perm_ref, o_ref, wg_scr, wl_scr, wd_scr):
    i = pl.program_id(0)
    valid = valid_ref[i] == 1

    @pl.when(first_ref[i] == 1)
    def _():
        perm = perm_ref[...]
        for g in range(wgu_ref.shape[2] // MXU_DIM):
            blk = wgu_ref[0, :, g * MXU_DIM:(g + 1) * MXU_DIM].astype(BF16)
            sp = jnp.dot(blk, perm, preferred_element_type=F32).astype(BF16)
            wg_scr[:, g * LANES:(g + 1) * LANES] = sp[:, :LANES]
            wl_scr[:, g * LANES:(g + 1) * LANES] = sp[:, LANES:]
        wd_scr[...] = wd_ref[0].astype(BF16)

    @pl.when(jnp.logical_not(valid))
    def _():
        o_ref[...] = jnp.zeros_like(o_ref)

    @pl.when(valid)
    def _():
        x = xs_ref[...].astype(BF16)
        glu = jnp.dot(x, wg_scr[...], preferred_element_type=F32) + bg_ref[0]
        lin = jnp.dot(x, wl_scr[...], preferred_element_type=F32) + bl_ref[0]
        glu = jnp.minimum(glu, SWIGLU_LIMIT)
        lin = jnp.clip(lin, -SWIGLU_LIMIT, SWIGLU_LIMIT)
        act = glu * jax.nn.sigmoid(SWIGLU_ALPHA * glu) * (lin + 1.0)
        o_ref[...] = jnp.dot(act.astype(BF16), wd_scr[...], preferred_element_type=F32) + bd_ref[0]


def _experts(xs, tile_blk, tile_exp, tile_valid, tile_first, wgu, bg, bl, wd, bd, perm):
    p, d = xs.shape
    de = wd.shape[1]
    tm = EXPERT_TILE
    per_e = lambda shape: pl.BlockSpec((1,) + shape, lambda i, blk, ex, va, fi: (ex[i], 0, 0))
    grid_spec = pltpu.PrefetchScalarGridSpec(
        num_scalar_prefetch=4,
        grid=(p // tm,),
        in_specs=[pl.BlockSpec((tm, d), lambda i, blk, ex, va, fi: (blk[i], 0)),
                  per_e((d, 2 * de)), per_e((1, de)), per_e((1, de)), per_e((de, d)), per_e((1, d)),
                  pl.BlockSpec((MXU_DIM, MXU_DIM), lambda i, blk, ex, va, fi: (0, 0))],
        out_specs=pl.BlockSpec((tm, d), lambda i, blk, ex, va, fi: (i, 0)),
        scratch_shapes=[pltpu.VMEM((d, de), BF16), pltpu.VMEM((d, de), BF16), pltpu.VMEM((de, d), BF16)],
    )
    return pl.pallas_call(
        _expert_kernel,
        grid_spec=grid_spec,
        out_shape=jax.ShapeDtypeStruct((p, d), F32),
        compiler_params=_cparams(("arbitrary",), EXPERT_VMEM_LIMIT),
        name="experts",
    )(tile_blk, tile_exp, tile_valid, tile_first, xs, wgu, bg, bl, wd, bd, perm)


def _combine_kernel(pos_ref, h_ref, gt_ref, p_ref, wp_ref, gp_ref, wg_ref, gf_ref, ys_ref, o_ref,
                    rows_scr, sem, *, tm):
    def issue(r, c):
        for k in range(TOP_K):
            pltpu.make_async_copy(ys_ref.at[pl.ds(pos_ref[0, 0, k * tm + r], 1), :],
                                  rows_scr.at[k, pl.ds(r, 1), :], sem).start()
        return c

    lax.fori_loop(0, tm, issue, 0)

    emb = jnp.dot(p_ref[...].astype(BF16), wp_ref[...], preferred_element_type=F32)
    emb = emb * lax.rsqrt(jnp.mean(emb * emb, axis=-1, keepdims=True) + NORM_EPS) * gp_ref[...]

    def drain(r, c):
        for k in range(TOP_K):
            pltpu.make_async_copy(ys_ref.at[pl.ds(pos_ref[0, 0, k * tm + r], 1), :],
                                  rows_scr.at[k, pl.ds(r, 1), :], sem).wait()
        return c

    lax.fori_loop(0, tm, drain, 0)

    gates = gt_ref[...]
    y = gates[:, 0:1] * rows_scr[0]
    for k in range(1, TOP_K):
        y = y + gates[:, k:k + 1] * rows_scr[k]
    h = h_ref[...] + y
    gate = jax.nn.sigmoid(jnp.dot(h.astype(BF16), wg_ref[...], preferred_element_type=F32))
    h = h + gate * emb
    o_ref[...] = h * lax.rsqrt(jnp.mean(h * h, axis=-1, keepdims=True) + NORM_EPS) * gf_ref[...]


def _combine(pos_tiles, h1, gates_t, p2, wp, gp, wg, gf, ys):
    n, d = h1.shape
    tm = TOKEN_TILE
    row = lambda w: pl.BlockSpec((tm, w), lambda i: (i, 0))
    full = lambda a: pl.BlockSpec(a.shape, lambda i: (0,) * a.ndim)
    kern = functools.partial(_combine_kernel, tm=tm)
    return pl.pallas_call(
        kern,
        grid=(n // tm,),
        in_specs=[pl.BlockSpec((1, 1, TOP_K * tm), lambda i: (i, 0, 0), memory_space=pltpu.SMEM),
                  row(d), row(TOP_K), row(p2.shape[1]), full(wp), full(gp), full(wg), full(gf),
                  pl.BlockSpec(memory_space=pl.ANY)],
        out_specs=row(d),
        out_shape=jax.ShapeDtypeStruct((n, d), F32),
        scratch_shapes=[pltpu.VMEM((TOP_K, tm, d), F32), pltpu.SemaphoreType.DMA(())],
        compiler_params=_cparams(("arbitrary",), VMEM_LIMIT),
        name="combine_ple",
    )(pos_tiles, h1, gates_t, p2, wp, gp, wg, gf, ys)


def _t5_bucket(rel):
    n = jnp.maximum(rel, 0)
    max_exact = NUM_BUCKETS // 2
    nf = jnp.maximum(n, 1).astype(F32)
    large = max_exact + (jnp.log(nf / max_exact) / math.log(MAX_DISTANCE / max_exact)
                         * (NUM_BUCKETS - max_exact)).astype(I32)
    large = jnp.minimum(large, NUM_BUCKETS - 1)
    return jnp.where(n < max_exact, n, large)


def _bias_tiles(rel_bias, tq):
    n_heads = rel_bias.shape[1]
    lo = -tq
    rel = jnp.arange(lo, 3 * tq)
    vec = rel_bias.astype(F32)[_t5_bucket(rel)] * LOG2E
    vec = jnp.where((rel >= 0)[:, None], vec, -jnp.inf).T
    length = vec.shape[1]
    rev = vec[:, ::-1]
    tiles = []
    for d in range(3):
        c = d * tq - lo
        o = length - c - tq
        win = rev[:, o:o + 2 * tq]
        skew = jnp.tile(win, (1, tq))[:, :tq * (2 * tq - 1)].reshape(n_heads, tq, 2 * tq - 1)
        tiles.append(skew[:, :, tq - 1:])
    tiles.append(jnp.full((n_heads, tq, tq), -jnp.inf, F32))
    return jnp.stack(tiles, axis=1)


def _far_bias_is_constant(tq):
    n = np.arange(tq + 1, 4 * tq).astype(np.float32)
    max_exact = NUM_BUCKETS // 2
    large = max_exact + (np.log(n / max_exact) / math.log(MAX_DISTANCE / max_exact)
                         * (NUM_BUCKETS - max_exact)).astype(np.int32)
    return bool(np.all(np.minimum(large, NUM_BUCKETS - 1) == NUM_BUCKETS - 1))


def _tile_pos(pos, tm):
    k, n = pos.shape
    return pos.reshape(k, n // tm, tm).transpose(1, 0, 2).reshape(n // tm, 1, k * tm)


def _split_perm():
    src = np.arange(MXU_DIM)
    dst = np.where(src % 2 == 0, src // 2, LANES + src // 2)
    perm = np.zeros((MXU_DIM, MXU_DIM), np.float32)
    perm[src, dst] = 1.0
    return jnp.asarray(perm, BF16)


def kernel(x, p, w_in, w_out, attn_norm, moe_norm, rel_bias, lambda_q1, lambda_k1, lambda_q2, lambda_k2, subln,
           router_w, router_b, w_gate_up, b_gate_up, w_down, b_down, ple_proj, ple_norm, ple_gate, final_norm):
    batch, seq, d = x.shape
    assert w_in.shape[0] == 1
    n = batch * seq
    n_experts = router_w.shape[2]
    n_diff = (d // 2) // (2 * HEAD_DIM)
    n_sb = (d - d // 2) // HEAD_DIM
    qk_w = n_diff * HEAD_DIM
    assert 2 * HEAD_DIM == LANES and n_sb % 2 == 0 and n_diff % 2 == 0
    assert seq % ATTN_TILE == 0 and n % INPROJ_TILE == 0 and n % TOKEN_TILE == 0
    assert _far_bias_is_constant(ATTN_TILE)
    lambda_init = 0.8 - 0.6 * math.exp(-0.3 * 0)
    scale = HEAD_DIM ** -0.5

    x2 = x.reshape(n, d)
    col_scale = np.ones((w_in.shape[2],), np.float32)
    col_scale[:2 * qk_w] = scale * LOG2E
    sb0 = 4 * qk_w + n_diff * DIFF_V_DIM
    col_scale[sb0:sb0 + n_sb * HEAD_DIM] = -scale * LOG2E
    w_all = (w_in[0] * col_scale[None, :]).astype(BF16)

    proj = _inproj(x2, attn_norm[0].reshape(1, d), w_all)

    o_diff = _diff_attention(
        proj, _bias_tiles(rel_bias, ATTN_TILE),
        lambda_q1[0].reshape(1, HEAD_DIM).astype(F32), lambda_k1[0].reshape(1, HEAD_DIM).astype(F32),
        lambda_q2[0].reshape(1, HEAD_DIM).astype(F32), lambda_k2[0].reshape(1, HEAD_DIM).astype(F32),
        subln[0].reshape(1, DIFF_V_DIM).astype(F32),
        batch=batch, seq=seq, n_heads=n_diff, lambda_init=lambda_init)

    t = ATTN_TILE
    u = (jnp.arange(t)[:, None] > jnp.arange(t)[None, :]).astype(BF16)
    o_sb = _sb_attention(proj, u, batch=batch, seq=seq, n_pairs=n_sb // 2, col0=sb0 // LANES)

    wo = w_out[0].astype(BF16)
    rw = router_w[0].T
    rwh = rw.astype(BF16)
    rwl = (rw - rwh.astype(F32)).astype(BF16)
    tt = TOKEN_TILE
    ut = (jnp.arange(tt)[:, None] < jnp.arange(tt)[None, :]).astype(BF16)
    h1, xn, top_i, gates, rank, counts = _outproj_router(
        o_diff, o_sb, x2, wo[:n_diff * DIFF_V_DIM], wo[n_diff * DIFF_V_DIM:], moe_norm[0].reshape(1, d),
        rwh, rwl, router_b[0].reshape(n_experts, 1).astype(F32), ut)

    te = EXPERT_TILE
    n_rows = n * TOP_K + n_experts * te
    n_tiles = n_rows // te
    cnt = counts[:, 0]
    tiles_e = (cnt + te - 1) // te
    tile_end = jnp.cumsum(tiles_e)
    tile_start = tile_end - tiles_e
    base = tile_start * te
    n_valid = tile_end[-1]
    tidx = jnp.arange(n_tiles, dtype=I32)
    tile_valid = (tidx < n_valid).astype(I32)
    tile_blk = jnp.minimum(tidx, n_valid - 1).astype(I32)
    tile_exp = jnp.minimum(jnp.sum(tile_blk[:, None] >= tile_end[None, :], axis=1), n_experts - 1).astype(I32)
    tile_first = (jnp.any((tidx[:, None] == tile_start[None, :]) & (tiles_e[None, :] > 0), axis=1)
                  & (tidx < n_valid)).astype(I32)
    eids = jnp.arange(n_experts, dtype=I32)[:, None, None]
    pos = (rank + jnp.sum(jnp.where(top_i[None] == eids, base[:, None, None], 0), axis=0)).astype(I32)
    pos_tiles = _tile_pos(pos, tt)
    fill_lo = (base + cnt).astype(I32)
    fill_hi = (tile_end * te).astype(I32)

    xs = _dispatch(xn, pos_tiles, fill_lo, fill_hi, n_valid.reshape(1).astype(I32), n_rows)

    bgu = b_gate_up[0].astype(F32)
    ys = _experts(xs, tile_blk, tile_exp, tile_valid, tile_first, w_gate_up[0],
                  bgu[:, None, 0::2], bgu[:, None, 1::2], w_down[0], b_down[0][:, None, :].astype(F32),
                  _split_perm())

    out = _combine(pos_tiles, h1, gates.T, p[0].reshape(n, -1), ple_proj[0].astype(BF16),
                   ple_norm[0].reshape(1, d), ple_gate[0].astype(BF16), final_norm.reshape(1, d), ys)
    return out.reshape(batch, seq, d)
```

```python
import functools
import math

import numpy as np
import jax
import jax.numpy as jnp
from jax import lax
from jax.experimental import pallas as pl
from jax.experimental.pallas import tpu as pltpu

F32 = jnp.float32
BF16 = jnp.bfloat16
I32 = jnp.int32

HEAD_DIM = 64
DIFF_V_DIM = 2 * HEAD_DIM
NUM_BUCKETS = 32
MAX_DISTANCE = 128
TOP_K = 4
SWIGLU_LIMIT = 7.0
SWIGLU_ALPHA = 1.702
NORM_EPS = 1e-6
LOG2E = math.log2(math.e)

LANES = 128
MXU_DIM = 256
ATTN_TILE = 256
SB_BLOCK = 256
TOKEN_TILE = 256
INPROJ_TILE = 512
EXPERT_TILE = 512
VMEM_LIMIT = 48 * 1024 * 1024
EXPERT_VMEM_LIMIT = 56 * 1024 * 1024


def _cparams(sem, vmem=None):
    return pltpu.CompilerParams(dimension_semantics=sem, vmem_limit_bytes=vmem)


def _inproj_kernel(x_ref, g_ref, w_ref, o_ref):
    x = x_ref[...]
    inv = lax.rsqrt(jnp.mean(x * x, axis=-1, keepdims=True) + NORM_EPS)
    hn = (x * inv * g_ref[...]).astype(BF16)
    o_ref[...] = jnp.dot(hn, w_ref[...], preferred_element_type=F32).astype(o_ref.dtype)


def _inproj(x2, g, w):
    n, d = x2.shape
    width = w.shape[1]
    tm = INPROJ_TILE
    return pl.pallas_call(
        _inproj_kernel,
        grid=(n // tm,),
        in_specs=[pl.BlockSpec((tm, d), lambda i: (i, 0)),
                  pl.BlockSpec((1, d), lambda i: (0, 0)),
                  pl.BlockSpec((d, width), lambda i: (0, 0))],
        out_specs=pl.BlockSpec((tm, width), lambda i: (i, 0)),
        out_shape=jax.ShapeDtypeStruct((n, width), BF16),
        compiler_params=_cparams(("arbitrary",), VMEM_LIMIT),
        name="inproj",
    )(x2, g, w)


_NT = (((1,), (1,)), ((), ()))


def _diff_attn_kernel(lq1_ref, lk1_ref, lq2_ref, lk2_ref, subln_ref, bias_ref, q1_ref, q2_ref, k1_ref, k2_ref,
                      v_ref, o_ref, s_scr, *, tq, lambda_init):
    qi = pl.program_id(2)
    lane = lax.broadcasted_iota(I32, (tq, LANES), 1)
    zq = jnp.zeros((tq, LANES), q1_ref.dtype)
    q1 = q1_ref[...]
    q2 = q2_ref[...]
    rows = []
    for hh in range(2):
        mine = (lane >= HEAD_DIM) == (hh == 1)
        rows.append(jnp.concatenate([jnp.where(mine, q1, zq), zq], axis=1))
        rows.append(jnp.concatenate([zq, jnp.where(mine, q2, zq)], axis=1))
    qq = jnp.concatenate(rows, axis=0)
    n_pairs = (qi + 2) // 2

    def key_tile(g, t):
        j = 2 * g + t
        d = qi - j
        return j, pl.multiple_of(j * tq, tq), jnp.where(d < 0, 3, jnp.minimum(d, 2))

    def scores(g, mx):
        for t in range(2):
            j, start, bi = key_tile(g, t)
            ba = bias_ref[0, bi]
            bb = bias_ref[1, bi]
            kk = jnp.concatenate([k1_ref[pl.ds(start, tq), :], k2_ref[pl.ds(start, tq), :]], axis=1)
            s = (lax.dot_general(qq, kk, _NT, preferred_element_type=F32)
                 + jnp.concatenate([ba, ba, bb, bb], axis=0))
            s_scr[j] = s
            mx = jnp.maximum(mx, jnp.maximum(s[:, :LANES], s[:, LANES:]))
        return mx

    mx = lax.fori_loop(0, n_pairs, scores, jnp.full((4 * tq, LANES), -jnp.inf, F32))
    m = jnp.max(mx, axis=1, keepdims=True)

    def weighted(g, carry):
        l_acc, acc = carry
        for t in range(2):
            j, start, _ = key_tile(g, t)
            p = jnp.exp2(s_scr[j] - m)
            l_acc = l_acc + (p[:, :LANES] + p[:, LANES:])
            acc = acc + jnp.dot(p.astype(BF16), v_ref[pl.ds(start, tq), :], preferred_element_type=F32)
        return l_acc, acc

    l_acc, acc = lax.fori_loop(0, n_pairs, weighted,
                               (jnp.zeros((4 * tq, LANES), F32), jnp.zeros((4 * tq, 2 * LANES), F32)))
    l = jnp.sum(l_acc, axis=1, keepdims=True)
    lam = (jnp.exp(jnp.sum(lq1_ref[...] * lk1_ref[...], axis=1, keepdims=True))
           - jnp.exp(jnp.sum(lq2_ref[...] * lk2_ref[...], axis=1, keepdims=True)) + lambda_init)
    outs = []
    for hh in range(2):
        r1, r2 = 2 * hh * tq, (2 * hh + 1) * tq
        a1 = acc[r1:r1 + tq, hh * LANES:(hh + 1) * LANES]
        a2 = acc[r2:r2 + tq, hh * LANES:(hh + 1) * LANES]
        o = a1 / l[r1:r1 + tq] - lam * (a2 / l[r2:r2 + tq])
        o = o * lax.rsqrt(jnp.mean(o * o, axis=-1, keepdims=True) + NORM_EPS) * subln_ref[...]
        outs.append((o * (1.0 - lambda_init)).astype(o_ref.dtype))
    o_ref[...] = jnp.concatenate(outs, axis=1)


def _diff_attention(proj, bias_tiles, lq1, lk1, lq2, lk2, subln, *, batch, seq, n_heads, lambda_init):
    n = proj.shape[0]
    tq = ATTN_TILE
    nq = seq // tq
    assert nq % 2 == 0
    nb = n_heads // 2
    small = lambda shape: pl.BlockSpec(shape, lambda b, h, i: (0,) * len(shape))
    kern = functools.partial(_diff_attn_kernel, tq=tq, lambda_init=lambda_init)
    qspec = lambda grp: pl.BlockSpec((tq, LANES), lambda b, h, i: (b * nq + i, grp * nb + h))
    kspec = lambda grp: pl.BlockSpec((seq, LANES), lambda b, h, i: (b, grp * nb + h))
    return pl.pallas_call(
        kern,
        grid=(batch, nb, nq),
        in_specs=[small((1, HEAD_DIM)), small((1, HEAD_DIM)), small((1, HEAD_DIM)), small((1, HEAD_DIM)),
                  small((1, DIFF_V_DIM)),
                  pl.BlockSpec((2, 4, tq, tq), lambda b, h, i: (h, 0, 0, 0)),
                  qspec(0), qspec(1), kspec(2), kspec(3),
                  pl.BlockSpec((seq, 2 * LANES), lambda b, h, i: (b, 2 * nb + h))],
        out_specs=pl.BlockSpec((tq, 2 * LANES), lambda b, h, i: (b * nq + i, h)),
        out_shape=jax.ShapeDtypeStruct((n, n_heads * DIFF_V_DIM), BF16),
        scratch_shapes=[pltpu.VMEM((nq, 4 * tq, tq), F32)],
        compiler_params=_cparams(("arbitrary", "arbitrary", "arbitrary"), VMEM_LIMIT),
        name="diff_attention",
    )(lq1, lk1, lq2, lk2, subln, bias_tiles, proj, proj, proj, proj, proj)


def _sb_attn_kernel(u_ref, q_ref, k_ref, v_ref, o_ref, *, tq):
    qi = pl.program_id(2)
    q = q_ref[...]
    n_heads = q.shape[1] // HEAD_DIM
    lane = lax.broadcasted_iota(I32, q.shape, 1)
    zero = jnp.zeros_like(q)
    qs = jnp.concatenate([jnp.where((lane >= hh * HEAD_DIM) & (lane < (hh + 1) * HEAD_DIM), q, zero)
                          for hh in range(n_heads)], axis=0)
    row = lax.broadcasted_iota(I32, (n_heads * tq, tq), 0)
    col = lax.broadcasted_iota(I32, (n_heads * tq, tq), 1)
    strict = col < (row & (tq - 1))
    u = u_ref[...]

    def tile(j, carry, acc, diag, valid):
        start = pl.multiple_of(j * tq, tq)
        kj = k_ref[pl.ds(start, tq), :]
        vj = v_ref[pl.ds(start, tq), :]
        n = lax.dot_general(qs, kj, _NT, preferred_element_type=F32)
        lk = jnp.minimum(n, 0.0) - jnp.log2(1.0 + jnp.exp2(-jnp.abs(n)))
        if diag:
            lk = jnp.where(strict, lk, 0.0)
        sfx = jnp.dot(lk.astype(BF16), u, preferred_element_type=F32)
        w = jnp.exp2(lk - n + (sfx + carry))
        if diag:
            w = jnp.where(strict, w, 0.0)
        if valid is not None:
            w = jnp.where(valid, w, 0.0)
        acc = acc + jnp.dot(w.astype(BF16), vj, preferred_element_type=F32)
        carry = carry + (sfx[:, :1] + lk[:, :1])
        return carry, acc

    def group(g, state, diag):
        ja = qi - 2 * g
        jb = ja - 1
        carry, acc = tile(ja, state[0], state[1], diag, None)
        return tile(jnp.maximum(jb, 0), carry, acc, False, jb >= 0)

    state = group(0, (jnp.zeros((n_heads * tq, 1), F32), jnp.zeros((n_heads * tq, q.shape[1]), F32)), True)
    acc = lax.fori_loop(1, (qi + 2) // 2, lambda g, st: group(g, st, False), state)[1]
    out = acc[:tq]
    for hh in range(1, n_heads):
        out = jnp.where(lane >= hh * HEAD_DIM, acc[hh * tq:(hh + 1) * tq], out)
    o_ref[...] = out.astype(o_ref.dtype)


def _sb_attention(proj, u, *, batch, seq, n_heads, col0):
    n = proj.shape[0]
    tq = ATTN_TILE
    nq = seq // tq
    nblk = n_heads * HEAD_DIM // SB_BLOCK
    kern = functools.partial(_sb_attn_kernel, tq=tq)
    return pl.pallas_call(
        kern,
        grid=(batch, nblk, nq),
        in_specs=[pl.BlockSpec((tq, tq), lambda b, p, i: (0, 0)),
                  pl.BlockSpec((tq, SB_BLOCK), lambda b, p, i: (b * nq + i, col0 + p)),
                  pl.BlockSpec((seq, SB_BLOCK), lambda b, p, i: (b, col0 + nblk + p)),
                  pl.BlockSpec((seq, SB_BLOCK), lambda b, p, i: (b, col0 + 2 * nblk + p))],
        out_specs=pl.BlockSpec((tq, SB_BLOCK), lambda b, p, i: (b * nq + i, p)),
        out_shape=jax.ShapeDtypeStruct((n, n_heads * HEAD_DIM), BF16),
        compiler_params=_cparams(("arbitrary", "arbitrary", "arbitrary"), VMEM_LIMIT),
        name="sb_attention",
    )(u, proj, proj, proj)


def _outproj_router_kernel(od_ref, os_ref, x_ref, wod_ref, wos_ref, g_ref, rwh_ref, rwl_ref, rb_ref, ut_ref,
                           h_ref, xn_ref, ti_ref, gt_ref, rk_ref, cnt_ref, cnt_scr, *, n_experts):
    @pl.when(pl.program_id(0) == 0)
    def _():
        cnt_scr[...] = jnp.zeros_like(cnt_scr)

    attn = (jnp.dot(od_ref[...], wod_ref[...], preferred_element_type=F32)
            + jnp.dot(os_ref[...], wos_ref[...], preferred_element_type=F32))
    h = x_ref[...] + attn
    h_ref[...] = h
    xn = h * lax.rsqrt(jnp.mean(h * h, axis=-1, keepdims=True) + NORM_EPS) * g_ref[...]
    xn_ref[...] = xn
    xh = xn.astype(BF16)
    xl = (xn - xh.astype(F32)).astype(BF16)
    rwh = rwh_ref[...]
    logits = (lax.dot_general(rwh, xh, _NT, preferred_element_type=F32)
              + lax.dot_general(rwh, xl, _NT, preferred_element_type=F32)
              + lax.dot_general(rwl_ref[...], xh, _NT, preferred_element_type=F32)
              + rb_ref[...])
    eidx = lax.broadcasted_iota(I32, logits.shape, 0)
    work = logits
    vals, idxs, sels = [], [], []
    for _ in range(TOP_K):
        mx = jnp.max(work, axis=0, keepdims=True)
        ix = jnp.min(jnp.where(work == mx, eidx, n_experts), axis=0, keepdims=True)
        sel = eidx == ix
        work = jnp.where(sel, -jnp.inf, work)
        vals.append(mx)
        idxs.append(ix)
        sels.append(sel)
    ex = [jnp.exp(v - vals[0]) for v in vals]
    den = ex[0] + ex[1] + ex[2] + ex[3]
    ti_ref[...] = jnp.concatenate(idxs, axis=0)
    gt_ref[...] = jnp.concatenate([e / den for e in ex], axis=0)
    onehot = jnp.where(sels[0] | sels[1] | sels[2] | sels[3], 1.0, 0.0)
    rank = jnp.dot(onehot.astype(BF16), ut_ref[...], preferred_element_type=F32) + cnt_scr[...]
    rk_ref[...] = jnp.concatenate(
        [jnp.sum(jnp.where(s, rank, 0.0), axis=0, keepdims=True) for s in sels], axis=0).astype(I32)
    cnt_scr[...] += jnp.sum(onehot, axis=1, keepdims=True)
    cnt_ref[...] = cnt_scr[...].astype(I32)


def _outproj_router(o_diff, o_sb, x2, wod, wos, g, rwh, rwl, rb, ut):
    n, d = x2.shape
    e = rwh.shape[0]
    tm = TOKEN_TILE
    row = lambda w: pl.BlockSpec((tm, w), lambda i: (i, 0))
    full = lambda a: pl.BlockSpec(a.shape, lambda i: (0,) * a.ndim)
    kcol = pl.BlockSpec((TOP_K, tm), lambda i: (0, i))
    kern = functools.partial(_outproj_router_kernel, n_experts=e)
    return pl.pallas_call(
        kern,
        grid=(n // tm,),
        in_specs=[row(o_diff.shape[1]), row(o_sb.shape[1]), row(d), full(wod), full(wos), full(g),
                  full(rwh), full(rwl), full(rb), full(ut)],
        out_specs=[row(d), row(d), kcol, kcol, kcol, pl.BlockSpec((e, 1), lambda i: (0, 0))],
        out_shape=[jax.ShapeDtypeStruct((n, d), F32), jax.ShapeDtypeStruct((n, d), F32),
                   jax.ShapeDtypeStruct((TOP_K, n), I32), jax.ShapeDtypeStruct((TOP_K, n), F32),
                   jax.ShapeDtypeStruct((TOP_K, n), I32), jax.ShapeDtypeStruct((e, 1), I32)],
        scratch_shapes=[pltpu.VMEM((e, 1), F32)],
        compiler_params=_cparams(("arbitrary",), VMEM_LIMIT),
        name="outproj_router",
    )(o_diff, o_sb, x2, wod, wos, g, rwh, rwl, rb, ut)


def _row_copy(src_ref, src_row, dst_ref, dst_row, sem):
    return pltpu.make_async_copy(src_ref.at[pl.ds(src_row, 1), :], dst_ref.at[pl.ds(dst_row, 1), :], sem)


def _dispatch_kernel(fill_lo_ref, fill_hi_ref, nvalid_ref, pos_ref, xn_ref, xs_ref, zero_scr, sems,
                     *, tm, te, n_experts, n_tiles):
    sem, tail_sem, fill_sem = sems.at[0], sems.at[1], sems.at[2]

    def issue(r, c):
        for k in range(TOP_K):
            _row_copy(xn_ref, r, xs_ref, pos_ref[0, 0, k * tm + r], sem).start(priority=k % 2)
        return c

    lax.fori_loop(0, tm, issue, 0)

    @pl.when(pl.program_id(0) == 0)
    def _():
        zero_scr[...] = jnp.zeros_like(zero_scr)

        def tail_copy(t):
            return pltpu.make_async_copy(zero_scr, xs_ref.at[pl.ds(pl.multiple_of(t * te, te), te), :],
                                         tail_sem)

        def tail(t, c):
            tail_copy(t).start()
            return c

        lax.fori_loop(nvalid_ref[0], n_tiles, tail, 0)

        def drain_tail(t, c):
            tail_copy(t).wait()
            return c

        lax.fori_loop(nvalid_ref[0], n_tiles, drain_tail, 0)
        for e in range(n_experts):
            def fill(r, c):
                _row_copy(zero_scr, 0, xs_ref, r, fill_sem).start()
                return c
            lax.fori_loop(fill_lo_ref[e], fill_hi_ref[e], fill, 0)
        for e in range(n_experts):
            def drain_fill(r, c):
                _row_copy(zero_scr, 0, xs_ref, r, fill_sem).wait()
                return c
            lax.fori_loop(fill_lo_ref[e], fill_hi_ref[e], drain_fill, 0)

    def drain(r, c):
        for k in range(TOP_K):
            _row_copy(xn_ref, r, xs_ref, pos_ref[0, 0, k * tm + r], sem).wait()
        return c

    lax.fori_loop(0, tm, drain, 0)


def _dispatch(xn, pos_tiles, fill_lo, fill_hi, n_valid, n_rows):
    n, d = xn.shape
    tm = TOKEN_TILE
    te = EXPERT_TILE
    e = fill_lo.shape[0]
    kern = functools.partial(_dispatch_kernel, tm=tm, te=te, n_experts=e, n_tiles=n_rows // te)
    grid_spec = pltpu.PrefetchScalarGridSpec(
        num_scalar_prefetch=3,
        grid=(n // tm,),
        in_specs=[pl.BlockSpec((1, 1, TOP_K * tm), lambda i, lo, hi, nv: (i, 0, 0), memory_space=pltpu.SMEM),
                  pl.BlockSpec((tm, d), lambda i, lo, hi, nv: (i, 0))],
        out_specs=pl.BlockSpec(memory_space=pl.ANY),
        scratch_shapes=[pltpu.VMEM((te, d), F32), pltpu.SemaphoreType.DMA((3,))],
    )
    return pl.pallas_call(
        kern,
        grid_spec=grid_spec,
        out_shape=jax.ShapeDtypeStruct((n_rows, d), F32),
        compiler_params=_cparams(("arbitrary",)),
        name="dispatch",
    )(fill_lo, fill_hi, n_valid, pos_tiles, xn)


def _expert_kernel(blk_ref, exp_ref, valid_ref, first_ref, xs_ref, wgu_ref, bg_ref, bl_ref, wd_ref, bd_ref,
                   perm_ref, o_ref, wg_scr, wl_scr, wd_scr):
    i = pl.program_id(0)
    valid = valid_ref[i] == 1

    @pl.when(first_ref[i] == 1)
    def _():
        perm = perm_ref[...]
        for g in range(wgu_ref.shape[2] // MXU_DIM):
            blk = wgu_ref[0, :, g * MXU_DIM:(g + 1) * MXU_DIM].astype(BF16)
            sp = jnp.dot(blk, perm, preferred_element_type=F32).astype(BF16)
            wg_scr[:, g * LANES:(g + 1) * LANES] = sp[:, :LANES]
            wl_scr[:, g * LANES:(g + 1) * LANES] = sp[:, LANES:]
        wd_scr[...] = wd_ref[0].astype(BF16)

    @pl.when(jnp.logical_not(valid))
    def _():
        o_ref[...] = jnp.zeros_like(o_ref)

    @pl.when(valid)
    def _():
        x = xs_ref[...].astype(BF16)
        glu = jnp.dot(x, wg_scr[...], preferred_element_type=F32) + bg_ref[0]
        lin = jnp.dot(x, wl_scr[...], preferred_element_type=F32) + bl_ref[0]
        glu = jnp.minimum(glu, SWIGLU_LIMIT)
        lin = jnp.clip(lin, -SWIGLU_LIMIT, SWIGLU_LIMIT)
        act = glu * jax.nn.sigmoid(SWIGLU_ALPHA * glu) * (lin + 1.0)
        o_ref[...] = jnp.dot(act.astype(BF16), wd_scr[...], preferred_element_type=F32) + bd_ref[0]


def _experts(xs, tile_blk, tile_exp, tile_valid, tile_first, wgu, bg, bl, wd, bd, perm):
    p, d = xs.shape
    de = wd.shape[1]
    tm = EXPERT_TILE
    per_e = lambda shape: pl.BlockSpec((1,) + shape, lambda i, blk, ex, va, fi: (ex[i], 0, 0))
    grid_spec = pltpu.PrefetchScalarGridSpec(
        num_scalar_prefetch=4,
        grid=(p // tm,),
        in_specs=[pl.BlockSpec((tm, d), lambda i, blk, ex, va, fi: (blk[i], 0)),
                  per_e((d, 2 * de)), per_e((1, de)), per_e((1, de)), per_e((de, d)), per_e((1, d)),
                  pl.BlockSpec((MXU_DIM, MXU_DIM), lambda i, blk, ex, va, fi: (0, 0))],
        out_specs=pl.BlockSpec((tm, d), lambda i, blk, ex, va, fi: (i, 0)),
        scratch_shapes=[pltpu.VMEM((d, de), BF16), pltpu.VMEM((d, de), BF16), pltpu.VMEM((de, d), BF16)],
    )
    return pl.pallas_call(
        _expert_kernel,
        grid_spec=grid_spec,
        out_shape=jax.ShapeDtypeStruct((p, d), F32),
        compiler_params=_cparams(("arbitrary",), EXPERT_VMEM_LIMIT),
        name="experts",
    )(tile_blk, tile_exp, tile_valid, tile_first, xs, wgu, bg, bl, wd, bd, perm)


def _combine_kernel(pos_ref, next_pos_ref, h_ref, gt_ref, p_ref, wp_ref, gp_ref, wg_ref, gf_ref, ys_ref, o_ref,
                    rows_scr, sems, *, tm, n_steps):
    i = pl.program_id(0)
    slot = i % 2

    def gather(idx_ref, s, start):
        def body(r, c):
            for k in range(TOP_K):
                cp = pltpu.make_async_copy(ys_ref.at[pl.ds(idx_ref[0, 0, k * tm + r], 1), :],
                                           rows_scr.at[s, k, pl.ds(r, 1), :], sems.at[s])
                if start:
                    cp.start(priority=k % 2)
                else:
                    cp.wait()
            return c
        lax.fori_loop(0, tm, body, 0)

    @pl.when(i == 0)
    def _():
        gather(pos_ref, slot, True)

    @pl.when(i + 1 < n_steps)
    def _():
        gather(next_pos_ref, 1 - slot, True)

    emb = jnp.dot(p_ref[...].astype(BF16), wp_ref[...], preferred_element_type=F32)
    emb = emb * lax.rsqrt(jnp.mean(emb * emb, axis=-1, keepdims=True) + NORM_EPS) * gp_ref[...]

    gather(pos_ref, slot, False)

    gates = gt_ref[...]
    y = gates[:, 0:1] * rows_scr[slot, 0]
    for k in range(1, TOP_K):
        y = y + gates[:, k:k + 1] * rows_scr[slot, k]
    h = h_ref[...] + y
    gate = jax.nn.sigmoid(jnp.dot(h.astype(BF16), wg_ref[...], preferred_element_type=F32))
    h = h + gate * emb
    o_ref[...] = h * lax.rsqrt(jnp.mean(h * h, axis=-1, keepdims=True) + NORM_EPS) * gf_ref[...]


def _combine(pos_tiles, h1, gates_t, p2, wp, gp, wg, gf, ys):
    n, d = h1.shape
    tm = TOKEN_TILE
    row = lambda w: pl.BlockSpec((tm, w), lambda i: (i, 0))
    full = lambda a: pl.BlockSpec(a.shape, lambda i: (0,) * a.ndim)
    n_steps = n // tm
    kern = functools.partial(_combine_kernel, tm=tm, n_steps=n_steps)
    pos_spec = lambda off: pl.BlockSpec((1, 1, TOP_K * tm), lambda i: (jnp.minimum(i + off, n_steps - 1), 0, 0),
                                        memory_space=pltpu.SMEM)
    return pl.pallas_call(
        kern,
        grid=(n_steps,),
        in_specs=[pos_spec(0), pos_spec(1),
                  row(d), row(TOP_K), row(p2.shape[1]), full(wp), full(gp), full(wg), full(gf),
                  pl.BlockSpec(memory_space=pl.ANY)],
        out_specs=row(d),
        out_shape=jax.ShapeDtypeStruct((n, d), F32),
        scratch_shapes=[pltpu.VMEM((2, TOP_K, tm, d), F32), pltpu.SemaphoreType.DMA((2,))],
        compiler_params=_cparams(("arbitrary",), VMEM_LIMIT),
        name="combine_ple",
    )(pos_tiles, pos_tiles, h1, gates_t, p2, wp, gp, wg, gf, ys)


def _t5_bucket(rel):
    n = jnp.maximum(rel, 0)
    max_exact = NUM_BUCKETS // 2
    nf = jnp.maximum(n, 1).astype(F32)
    large = max_exact + (jnp.log(nf / max_exact) / math.log(MAX_DISTANCE / max_exact)
                         * (NUM_BUCKETS - max_exact)).astype(I32)
    large = jnp.minimum(large, NUM_BUCKETS - 1)
    return jnp.where(n < max_exact, n, large)


def _bias_tiles(rel_bias, tq):
    n_heads = rel_bias.shape[1]
    lo = -tq
    rel = jnp.arange(lo, 3 * tq)
    vec = rel_bias.astype(F32)[_t5_bucket(rel)] * LOG2E
    vec = jnp.where((rel >= 0)[:, None], vec, -jnp.inf).T
    length = vec.shape[1]
    rev = vec[:, ::-1]
    tiles = []
    for d in range(3):
        c = d * tq - lo
        o = length - c - tq
        win = rev[:, o:o + 2 * tq]
        skew = jnp.tile(win, (1, tq))[:, :tq * (2 * tq - 1)].reshape(n_heads, tq, 2 * tq - 1)
        tiles.append(skew[:, :, tq - 1:])
    tiles.append(jnp.full((n_heads, tq, tq), -jnp.inf, F32))
    return jnp.stack(tiles, axis=1)


def _far_bias_is_constant(tq):
    n = np.arange(tq + 1, 4 * tq).astype(np.float32)
    max_exact = NUM_BUCKETS // 2
    large = max_exact + (np.log(n / max_exact) / math.log(MAX_DISTANCE / max_exact)
                         * (NUM_BUCKETS - max_exact)).astype(np.int32)
    return bool(np.all(np.minimum(large, NUM_BUCKETS - 1) == NUM_BUCKETS - 1))


def _tile_pos(pos, tm):
    k, n = pos.shape
    return pos.reshape(k, n // tm, tm).transpose(1, 0, 2).reshape(n // tm, 1, k * tm)


def _split_perm():
    src = np.arange(MXU_DIM)
    dst = np.where(src % 2 == 0, src // 2, LANES + src // 2)
    perm = np.zeros((MXU_DIM, MXU_DIM), np.float32)
    perm[src, dst] = 1.0
    return jnp.asarray(perm, BF16)


def kernel(x, p, w_in, w_out, attn_norm, moe_norm, rel_bias, lambda_q1, lambda_k1, lambda_q2, lambda_k2, subln,
           router_w, router_b, w_gate_up, b_gate_up, w_down, b_down, ple_proj, ple_norm, ple_gate, final_norm):
    batch, seq, d = x.shape
    assert w_in.shape[0] == 1
    n = batch * seq
    n_experts = router_w.shape[2]
    n_diff = (d // 2) // (2 * HEAD_DIM)
    n_sb = (d - d // 2) // HEAD_DIM
    qk_w = n_diff * HEAD_DIM
    sb0 = 4 * qk_w + n_diff * DIFF_V_DIM
    assert 2 * HEAD_DIM == LANES and (n_sb * HEAD_DIM) % SB_BLOCK == 0 and n_diff % 2 == 0
    assert sb0 % SB_BLOCK == 0 and ATTN_TILE & (ATTN_TILE - 1) == 0
    assert seq % ATTN_TILE == 0 and n % INPROJ_TILE == 0 and n % TOKEN_TILE == 0
    assert _far_bias_is_constant(ATTN_TILE)
    lambda_init = 0.8 - 0.6 * math.exp(-0.3 * 0)
    scale = HEAD_DIM ** -0.5

    x2 = x.reshape(n, d)
    col_scale = np.ones((w_in.shape[2],), np.float32)
    col_scale[:2 * qk_w] = scale * LOG2E
    col_scale[sb0:sb0 + n_sb * HEAD_DIM] = -scale * LOG2E
    w_all = (w_in[0] * col_scale[None, :]).astype(BF16)

    proj = _inproj(x2, attn_norm[0].reshape(1, d), w_all)

    o_diff = _diff_attention(
        proj, _bias_tiles(rel_bias, ATTN_TILE),
        lambda_q1[0].reshape(1, HEAD_DIM).astype(F32), lambda_k1[0].reshape(1, HEAD_DIM).astype(F32),
        lambda_q2[0].reshape(1, HEAD_DIM).astype(F32), lambda_k2[0].reshape(1, HEAD_DIM).astype(F32),
        subln[0].reshape(1, DIFF_V_DIM).astype(F32),
        batch=batch, seq=seq, n_heads=n_diff, lambda_init=lambda_init)

    t = ATTN_TILE
    u = (jnp.arange(t)[:, None] > jnp.arange(t)[None, :]).astype(BF16)
    o_sb = _sb_attention(proj, u, batch=batch, seq=seq, n_heads=n_sb, col0=sb0 // SB_BLOCK)

    wo = w_out[0].astype(BF16)
    rw = router_w[0].T
    rwh = rw.astype(BF16)
    rwl = (rw - rwh.astype(F32)).astype(BF16)
    tt = TOKEN_TILE
    ut = (jnp.arange(tt)[:, None] < jnp.arange(tt)[None, :]).astype(BF16)
    h1, xn, top_i, gates, rank, counts = _outproj_router(
        o_diff, o_sb, x2, wo[:n_diff * DIFF_V_DIM], wo[n_diff * DIFF_V_DIM:], moe_norm[0].reshape(1, d),
        rwh, rwl, router_b[0].reshape(n_experts, 1).astype(F32), ut)

    te = EXPERT_TILE
    n_rows = n * TOP_K + n_experts * te
    n_tiles = n_rows // te
    cnt = counts[:, 0]
    tiles_e = (cnt + te - 1) // te
    tile_end = jnp.cumsum(tiles_e)
    tile_start = tile_end - tiles_e
    base = tile_start * te
    n_valid = tile_end[-1]
    tidx = jnp.arange(n_tiles, dtype=I32)
    tile_valid = (tidx < n_valid).astype(I32)
    tile_blk = jnp.minimum(tidx, n_valid - 1).astype(I32)
    tile_exp = jnp.minimum(jnp.sum(tile_blk[:, None] >= tile_end[None, :], axis=1), n_experts - 1).astype(I32)
    tile_first = (jnp.any((tidx[:, None] == tile_start[None, :]) & (tiles_e[None, :] > 0), axis=1)
                  & (tidx < n_valid)).astype(I32)
    eids = jnp.arange(n_experts, dtype=I32)[:, None, None]
    pos = (rank + jnp.sum(jnp.where(top_i[None] == eids, base[:, None, None], 0), axis=0)).astype(I32)
    pos_tiles = _tile_pos(pos, tt)
    fill_lo = (base + cnt).astype(I32)
    fill_hi = (tile_end * te).astype(I32)

    xs = _dispatch(xn, pos_tiles, fill_lo, fill_hi, n_valid.reshape(1).astype(I32), n_rows)

    bgu = b_gate_up[0].astype(F32)
    ys = _experts(xs, tile_blk, tile_exp, tile_valid, tile_first, w_gate_up[0],
                  bgu[:, None, 0::2], bgu[:, None, 1::2], w_down[0], b_down[0][:, None, :].astype(F32),
                  _split_perm())

    out = _combine(pos_tiles, h1, gates.T, p[0].reshape(n, -1), ple_proj[0].astype(BF16),
                   ple_norm[0].reshape(1, d), ple_gate[0].astype(BF16), final_norm.reshape(1, d), ys)
    return out.reshape(batch, seq, d)
```

```python
import functools
import math

import numpy as np
import jax
import jax.numpy as jnp
from jax import lax
from jax.experimental import pallas as pl
from jax.experimental.pallas import tpu as pltpu

F32 = jnp.float32
BF16 = jnp.bfloat16
I32 = jnp.int32

HEAD_DIM = 64
DIFF_V_DIM = 2 * HEAD_DIM
NUM_BUCKETS = 32
MAX_DISTANCE = 128
TOP_K = 4
SWIGLU_LIMIT = 7.0
SWIGLU_ALPHA = 1.702
NORM_EPS = 1e-6
LOG2E = math.log2(math.e)

LANES = 128
SUBLANES = 8
MXU_DIM = 256
ATTN_TILE = 256
SB_BLOCK = 256
TOKEN_TILE = 256
INPROJ_TILE = 512
EXPERT_TILE = 512
VMEM_LIMIT = 48 * 1024 * 1024
EXPERT_VMEM_LIMIT = 56 * 1024 * 1024


def _cparams(sem, vmem=None):
    return pltpu.CompilerParams(dimension_semantics=sem, vmem_limit_bytes=vmem)


def _inproj_kernel(x_ref, g_ref, w_ref, o_ref):
    x = x_ref[...]
    inv = lax.rsqrt(jnp.mean(x * x, axis=-1, keepdims=True) + NORM_EPS)
    hn = (x * inv * g_ref[...]).astype(BF16)
    o_ref[...] = jnp.dot(hn, w_ref[...], preferred_element_type=F32).astype(o_ref.dtype)


def _inproj(x2, g, w):
    n, d = x2.shape
    width = w.shape[1]
    tm = INPROJ_TILE
    return pl.pallas_call(
        _inproj_kernel,
        grid=(n // tm,),
        in_specs=[pl.BlockSpec((tm, d), lambda i: (i, 0)),
                  pl.BlockSpec((1, d), lambda i: (0, 0)),
                  pl.BlockSpec((d, width), lambda i: (0, 0))],
        out_specs=pl.BlockSpec((tm, width), lambda i: (i, 0)),
        out_shape=jax.ShapeDtypeStruct((n, width), BF16),
        compiler_params=_cparams(("arbitrary",), VMEM_LIMIT),
        name="inproj",
    )(x2, g, w)


_NT = (((1,), (1,)), ((), ()))


def _diff_attn_kernel(lq1_ref, lk1_ref, lq2_ref, lk2_ref, subln_ref, bias_ref, q1_ref, q2_ref, k1_ref, k2_ref,
                      v_ref, o_ref, s_scr, *, tq, lambda_init):
    qi = pl.program_id(2)
    lane = lax.broadcasted_iota(I32, (tq, LANES), 1)
    zq = jnp.zeros((tq, LANES), q1_ref.dtype)
    q1 = q1_ref[...]
    q2 = q2_ref[...]
    rows = []
    for hh in range(2):
        mine = (lane >= HEAD_DIM) == (hh == 1)
        rows.append(jnp.concatenate([jnp.where(mine, q1, zq), zq], axis=1))
        rows.append(jnp.concatenate([zq, jnp.where(mine, q2, zq)], axis=1))
    qq = jnp.concatenate(rows, axis=0)
    n_pairs = (qi + 2) // 2

    def key_tile(g, t):
        j = 2 * g + t
        d = qi - j
        return j, pl.multiple_of(j * tq, tq), jnp.where(d < 0, 3, jnp.minimum(d, 2))

    def scores(g, mx):
        for t in range(2):
            j, start, bi = key_tile(g, t)
            ba = bias_ref[0, bi]
            bb = bias_ref[1, bi]
            kk = jnp.concatenate([k1_ref[pl.ds(start, tq), :], k2_ref[pl.ds(start, tq), :]], axis=1)
            s = (lax.dot_general(qq, kk, _NT, preferred_element_type=F32)
                 + jnp.concatenate([ba, ba, bb, bb], axis=0))
            s_scr[j] = s
            mx = jnp.maximum(mx, jnp.maximum(s[:, :LANES], s[:, LANES:]))
        return mx

    mx = lax.fori_loop(0, n_pairs, scores, jnp.full((4 * tq, LANES), -jnp.inf, F32))
    m = jnp.max(mx, axis=1, keepdims=True)

    def weighted(g, carry):
        l_acc, acc = carry
        for t in range(2):
            j, start, _ = key_tile(g, t)
            p = jnp.exp2(s_scr[j] - m)
            l_acc = l_acc + (p[:, :LANES] + p[:, LANES:])
            acc = acc + jnp.dot(p.astype(BF16), v_ref[pl.ds(start, tq), :], preferred_element_type=F32)
        return l_acc, acc

    l_acc, acc = lax.fori_loop(0, n_pairs, weighted,
                               (jnp.zeros((4 * tq, LANES), F32), jnp.zeros((4 * tq, 2 * LANES), F32)))
    l = jnp.sum(l_acc, axis=1, keepdims=True)
    lam = (jnp.exp(jnp.sum(lq1_ref[...] * lk1_ref[...], axis=1, keepdims=True))
           - jnp.exp(jnp.sum(lq2_ref[...] * lk2_ref[...], axis=1, keepdims=True)) + lambda_init)
    outs = []
    for hh in range(2):
        r1, r2 = 2 * hh * tq, (2 * hh + 1) * tq
        a1 = acc[r1:r1 + tq, hh * LANES:(hh + 1) * LANES]
        a2 = acc[r2:r2 + tq, hh * LANES:(hh + 1) * LANES]
        o = a1 / l[r1:r1 + tq] - lam * (a2 / l[r2:r2 + tq])
        o = o * lax.rsqrt(jnp.mean(o * o, axis=-1, keepdims=True) + NORM_EPS) * subln_ref[...]
        outs.append((o * (1.0 - lambda_init)).astype(o_ref.dtype))
    o_ref[...] = jnp.concatenate(outs, axis=1)


def _diff_attention(proj, bias_tiles, lq1, lk1, lq2, lk2, subln, *, batch, seq, n_heads, lambda_init):
    n = proj.shape[0]
    tq = ATTN_TILE
    nq = seq // tq
    assert nq % 2 == 0
    nb = n_heads // 2
    small = lambda shape: pl.BlockSpec(shape, lambda b, h, i: (0,) * len(shape))
    kern = functools.partial(_diff_attn_kernel, tq=tq, lambda_init=lambda_init)
    qspec = lambda grp: pl.BlockSpec((tq, LANES), lambda b, h, i: (b * nq + i, grp * nb + h))
    kspec = lambda grp: pl.BlockSpec((seq, LANES), lambda b, h, i: (b, grp * nb + h))
    return pl.pallas_call(
        kern,
        grid=(batch, nb, nq),
        in_specs=[small((1, HEAD_DIM)), small((1, HEAD_DIM)), small((1, HEAD_DIM)), small((1, HEAD_DIM)),
                  small((1, DIFF_V_DIM)),
                  pl.BlockSpec((2, 4, tq, tq), lambda b, h, i: (h, 0, 0, 0)),
                  qspec(0), qspec(1), kspec(2), kspec(3),
                  pl.BlockSpec((seq, 2 * LANES), lambda b, h, i: (b, 2 * nb + h))],
        out_specs=pl.BlockSpec((tq, 2 * LANES), lambda b, h, i: (b * nq + i, h)),
        out_shape=jax.ShapeDtypeStruct((n, n_heads * DIFF_V_DIM), BF16),
        scratch_shapes=[pltpu.VMEM((nq, 4 * tq, tq), F32)],
        compiler_params=_cparams(("arbitrary", "arbitrary", "arbitrary"), VMEM_LIMIT),
        name="diff_attention",
    )(lq1, lk1, lq2, lk2, subln, bias_tiles, proj, proj, proj, proj, proj)


def _sb_attn_kernel(u_ref, q_ref, k_ref, v_ref, o_ref, *, tq):
    qi = pl.program_id(2)
    q = q_ref[...]
    n_heads = q.shape[1] // HEAD_DIM
    lane = lax.broadcasted_iota(I32, q.shape, 1)
    zero = jnp.zeros_like(q)
    qs = jnp.concatenate([jnp.where((lane >= hh * HEAD_DIM) & (lane < (hh + 1) * HEAD_DIM), q, zero)
                          for hh in range(n_heads)], axis=0)
    row = lax.broadcasted_iota(I32, (n_heads * tq, tq), 0)
    col = lax.broadcasted_iota(I32, (n_heads * tq, tq), 1)
    strict = col < (row & (tq - 1))
    u = u_ref[...]

    def tile(j, carry, acc, diag, valid):
        start = pl.multiple_of(j * tq, tq)
        kj = k_ref[pl.ds(start, tq), :]
        vj = v_ref[pl.ds(start, tq), :]
        n = lax.dot_general(qs, kj, _NT, preferred_element_type=F32)
        lk = jnp.minimum(n, 0.0) - jnp.log2(1.0 + jnp.exp2(-jnp.abs(n)))
        if diag:
            lk = jnp.where(strict, lk, 0.0)
        sfx = jnp.dot(lk.astype(BF16), u, preferred_element_type=F32)
        w = jnp.exp2(lk - n + (sfx + carry))
        if diag:
            w = jnp.where(strict, w, 0.0)
        if valid is not None:
            w = jnp.where(valid, w, 0.0)
        acc = acc + jnp.dot(w.astype(BF16), vj, preferred_element_type=F32)
        carry = carry + (sfx[:, :1] + lk[:, :1])
        return carry, acc

    def group(g, state, diag):
        ja = qi - 2 * g
        jb = ja - 1
        carry, acc = tile(ja, state[0], state[1], diag, None)
        return tile(jnp.maximum(jb, 0), carry, acc, False, jb >= 0)

    state = group(0, (jnp.zeros((n_heads * tq, 1), F32), jnp.zeros((n_heads * tq, q.shape[1]), F32)), True)
    acc = lax.fori_loop(1, (qi + 2) // 2, lambda g, st: group(g, st, False), state)[1]
    out = acc[:tq]
    for hh in range(1, n_heads):
        out = jnp.where(lane >= hh * HEAD_DIM, acc[hh * tq:(hh + 1) * tq], out)
    o_ref[...] = out.astype(o_ref.dtype)


def _sb_attention(proj, u, *, batch, seq, n_heads, col0):
    n = proj.shape[0]
    tq = ATTN_TILE
    nq = seq // tq
    nblk = n_heads * HEAD_DIM // SB_BLOCK
    kern = functools.partial(_sb_attn_kernel, tq=tq)
    return pl.pallas_call(
        kern,
        grid=(batch, nblk, nq),
        in_specs=[pl.BlockSpec((tq, tq), lambda b, p, i: (0, 0)),
                  pl.BlockSpec((tq, SB_BLOCK), lambda b, p, i: (b * nq + i, col0 + p)),
                  pl.BlockSpec((seq, SB_BLOCK), lambda b, p, i: (b, col0 + nblk + p)),
                  pl.BlockSpec((seq, SB_BLOCK), lambda b, p, i: (b, col0 + 2 * nblk + p))],
        out_specs=pl.BlockSpec((tq, SB_BLOCK), lambda b, p, i: (b * nq + i, p)),
        out_shape=jax.ShapeDtypeStruct((n, n_heads * HEAD_DIM), BF16),
        compiler_params=_cparams(("arbitrary", "arbitrary", "arbitrary"), VMEM_LIMIT),
        name="sb_attention",
    )(u, proj, proj, proj)


def _outproj_router_kernel(od_ref, os_ref, x_ref, wod_ref, wos_ref, g_ref, rwh_ref, rwl_ref, rb_ref, ut_ref,
                           h_ref, xn_ref, ti_ref, gt_ref, rk_ref, cnt_ref, cnt_scr, *, n_experts):
    @pl.when(pl.program_id(0) == 0)
    def _():
        cnt_scr[...] = jnp.zeros_like(cnt_scr)

    attn = (jnp.dot(od_ref[...], wod_ref[...], preferred_element_type=F32)
            + jnp.dot(os_ref[...], wos_ref[...], preferred_element_type=F32))
    h = x_ref[...] + attn
    h_ref[...] = h
    xn = h * lax.rsqrt(jnp.mean(h * h, axis=-1, keepdims=True) + NORM_EPS) * g_ref[...]
    xn_ref[...] = xn
    xh = xn.astype(BF16)
    xl = (xn - xh.astype(F32)).astype(BF16)
    rwh = rwh_ref[...]
    logits = (lax.dot_general(rwh, xh, _NT, preferred_element_type=F32)
              + lax.dot_general(rwh, xl, _NT, preferred_element_type=F32)
              + lax.dot_general(rwl_ref[...], xh, _NT, preferred_element_type=F32)
              + rb_ref[...])
    eidx = lax.broadcasted_iota(I32, logits.shape, 0)
    work = logits
    vals, idxs, sels = [], [], []
    for _ in range(TOP_K):
        mx = jnp.max(work, axis=0, keepdims=True)
        ix = jnp.min(jnp.where(work == mx, eidx, n_experts), axis=0, keepdims=True)
        sel = eidx == ix
        work = jnp.where(sel, -jnp.inf, work)
        vals.append(mx)
        idxs.append(ix)
        sels.append(sel)
    ex = [jnp.exp(v - vals[0]) for v in vals]
    den = ex[0] + ex[1] + ex[2] + ex[3]
    ti_ref[...] = jnp.concatenate(idxs, axis=0)
    gt_ref[...] = jnp.concatenate([e / den for e in ex], axis=0)
    onehot = jnp.where(sels[0] | sels[1] | sels[2] | sels[3], 1.0, 0.0)
    rank = jnp.dot(onehot.astype(BF16), ut_ref[...], preferred_element_type=F32) + cnt_scr[...]
    rk_ref[...] = jnp.concatenate(
        [jnp.sum(jnp.where(s, rank, 0.0), axis=0, keepdims=True) for s in sels], axis=0).astype(I32)
    cnt_scr[...] += jnp.sum(onehot, axis=1, keepdims=True)
    cnt_ref[...] = cnt_scr[...].astype(I32)


def _outproj_router(o_diff, o_sb, x2, wod, wos, g, rwh, rwl, rb, ut):
    n, d = x2.shape
    e = rwh.shape[0]
    tm = TOKEN_TILE
    row = lambda w: pl.BlockSpec((tm, w), lambda i: (i, 0))
    full = lambda a: pl.BlockSpec(a.shape, lambda i: (0,) * a.ndim)
    kcol = pl.BlockSpec((TOP_K, tm), lambda i: (0, i))
    kern = functools.partial(_outproj_router_kernel, n_experts=e)
    return pl.pallas_call(
        kern,
        grid=(n // tm,),
        in_specs=[row(o_diff.shape[1]), row(o_sb.shape[1]), row(d), full(wod), full(wos), full(g),
                  full(rwh), full(rwl), full(rb), full(ut)],
        out_specs=[row(d), row(d), kcol, kcol, kcol, pl.BlockSpec((e, 1), lambda i: (0, 0))],
        out_shape=[jax.ShapeDtypeStruct((n, d), F32), jax.ShapeDtypeStruct((n, d), F32),
                   jax.ShapeDtypeStruct((TOP_K, n), I32), jax.ShapeDtypeStruct((TOP_K, n), F32),
                   jax.ShapeDtypeStruct((TOP_K, n), I32), jax.ShapeDtypeStruct((e, 1), I32)],
        scratch_shapes=[pltpu.VMEM((e, 1), F32)],
        compiler_params=_cparams(("arbitrary",), VMEM_LIMIT),
        name="outproj_router",
    )(o_diff, o_sb, x2, wod, wos, g, rwh, rwl, rb, ut)


def _to_token_major(dst_ref, x, n_tok):
    nc = x.shape[1] // LANES
    for c in range(nc):
        dst_ref[pl.ds(c, n_tok, stride=nc), :] = x[:, c * LANES:(c + 1) * LANES]


def _from_token_major(src_ref, n_tok, nc):
    return jnp.concatenate([src_ref[pl.ds(c, n_tok, stride=nc), :] for c in range(nc)], axis=1)


def _row_copy(src_ref, src_row, dst_ref, dst_row, sem, nc):
    return pltpu.make_async_copy(src_ref.at[pl.ds(pl.multiple_of(src_row * nc, nc), nc), :],
                                 dst_ref.at[pl.ds(pl.multiple_of(dst_row * nc, nc), nc), :], sem)


def _dispatch_kernel(fill_lo_ref, fill_hi_ref, nvalid_ref, pos_ref, xn_ref, xs_ref, tok_scr, zero_scr, sems,
                     *, tm, te, nc, n_experts, n_tiles):
    sem, tail_sem, fill_sem = sems.at[0], sems.at[1], sems.at[2]
    _to_token_major(tok_scr, xn_ref[...], tm)

    def issue(r, c):
        for k in range(TOP_K):
            _row_copy(tok_scr, r, xs_ref, pos_ref[0, 0, k * tm + r], sem, nc).start(priority=k % 2)
        return c

    lax.fori_loop(0, tm, issue, 0)

    @pl.when(pl.program_id(0) == 0)
    def _():
        zero_scr[...] = jnp.zeros_like(zero_scr)

        def tail_copy(t):
            return pltpu.make_async_copy(
                zero_scr, xs_ref.at[pl.ds(pl.multiple_of(t * te * nc, te * nc), te * nc), :], tail_sem)

        def tail(t, c):
            tail_copy(t).start()
            return c

        lax.fori_loop(nvalid_ref[0], n_tiles, tail, 0)

        def drain_tail(t, c):
            tail_copy(t).wait()
            return c

        lax.fori_loop(nvalid_ref[0], n_tiles, drain_tail, 0)
        for e in range(n_experts):
            def fill(r, c):
                _row_copy(zero_scr, 0, xs_ref, r, fill_sem, nc).start()
                return c
            lax.fori_loop(fill_lo_ref[e], fill_hi_ref[e], fill, 0)
        for e in range(n_experts):
            def drain_fill(r, c):
                _row_copy(zero_scr, 0, xs_ref, r, fill_sem, nc).wait()
                return c
            lax.fori_loop(fill_lo_ref[e], fill_hi_ref[e], drain_fill, 0)

    def drain(r, c):
        for k in range(TOP_K):
            _row_copy(tok_scr, r, xs_ref, pos_ref[0, 0, k * tm + r], sem, nc).wait()
        return c

    lax.fori_loop(0, tm, drain, 0)


def _dispatch(xn, pos_tiles, fill_lo, fill_hi, n_valid, n_rows):
    n, d = xn.shape
    tm = TOKEN_TILE
    te = EXPERT_TILE
    nc = d // LANES
    assert nc % SUBLANES == 0
    e = fill_lo.shape[0]
    kern = functools.partial(_dispatch_kernel, tm=tm, te=te, nc=nc, n_experts=e, n_tiles=n_rows // te)
    grid_spec = pltpu.PrefetchScalarGridSpec(
        num_scalar_prefetch=3,
        grid=(n // tm,),
        in_specs=[pl.BlockSpec((1, 1, TOP_K * tm), lambda i, lo, hi, nv: (i, 0, 0), memory_space=pltpu.SMEM),
                  pl.BlockSpec((tm, d), lambda i, lo, hi, nv: (i, 0))],
        out_specs=pl.BlockSpec(memory_space=pl.ANY),
        scratch_shapes=[pltpu.VMEM((tm * nc, LANES), F32), pltpu.VMEM((te * nc, LANES), F32),
                        pltpu.SemaphoreType.DMA((3,))],
    )
    return pl.pallas_call(
        kern,
        grid_spec=grid_spec,
        out_shape=jax.ShapeDtypeStruct((n_rows * nc, LANES), F32),
        compiler_params=_cparams(("arbitrary",)),
        name="dispatch",
    )(fill_lo, fill_hi, n_valid, pos_tiles, xn)


def _expert_kernel(blk_ref, exp_ref, valid_ref, first_ref, xs_ref, wgu_ref, bg_ref, bl_ref, wd_ref, bd_ref,
                   perm_ref, o_ref, wg_scr, wl_scr, wd_scr, *, te):
    i = pl.program_id(0)
    nc = wd_ref.shape[2] // LANES
    valid = valid_ref[i] == 1

    @pl.when(first_ref[i] == 1)
    def _():
        perm = perm_ref[...]
        for g in range(wgu_ref.shape[2] // MXU_DIM):
            blk = wgu_ref[0, :, g * MXU_DIM:(g + 1) * MXU_DIM].astype(BF16)
            sp = jnp.dot(blk, perm, preferred_element_type=F32).astype(BF16)
            wg_scr[:, g * LANES:(g + 1) * LANES] = sp[:, :LANES]
            wl_scr[:, g * LANES:(g + 1) * LANES] = sp[:, LANES:]
        wd_scr[...] = wd_ref[0].astype(BF16)

    @pl.when(jnp.logical_not(valid))
    def _():
        o_ref[...] = jnp.zeros_like(o_ref)

    @pl.when(valid)
    def _():
        x = _from_token_major(xs_ref, te, nc).astype(BF16)
        glu = jnp.dot(x, wg_scr[...], preferred_element_type=F32) + bg_ref[0]
        lin = jnp.dot(x, wl_scr[...], preferred_element_type=F32) + bl_ref[0]
        glu = jnp.minimum(glu, SWIGLU_LIMIT)
        lin = jnp.clip(lin, -SWIGLU_LIMIT, SWIGLU_LIMIT)
        act = glu * jax.nn.sigmoid(SWIGLU_ALPHA * glu) * (lin + 1.0)
        out = jnp.dot(act.astype(BF16), wd_scr[...], preferred_element_type=F32) + bd_ref[0]
        _to_token_major(o_ref, out, te)


def _experts(xs, tile_blk, tile_exp, tile_valid, tile_first, wgu, bg, bl, wd, bd, perm):
    de, d = wd.shape[1], wd.shape[2]
    nc = d // LANES
    te = EXPERT_TILE
    n_tiles = xs.shape[0] // (te * nc)
    per_e = lambda shape: pl.BlockSpec((1,) + shape, lambda i, blk, ex, va, fi: (ex[i], 0, 0))
    grid_spec = pltpu.PrefetchScalarGridSpec(
        num_scalar_prefetch=4,
        grid=(n_tiles,),
        in_specs=[pl.BlockSpec((te * nc, LANES), lambda i, blk, ex, va, fi: (blk[i], 0)),
                  per_e((d, 2 * de)), per_e((1, de)), per_e((1, de)), per_e((de, d)), per_e((1, d)),
                  pl.BlockSpec((MXU_DIM, MXU_DIM), lambda i, blk, ex, va, fi: (0, 0))],
        out_specs=pl.BlockSpec((te * nc, LANES), lambda i, blk, ex, va, fi: (i, 0)),
        scratch_shapes=[pltpu.VMEM((d, de), BF16), pltpu.VMEM((d, de), BF16), pltpu.VMEM((de, d), BF16)],
    )
    return pl.pallas_call(
        functools.partial(_expert_kernel, te=te),
        grid_spec=grid_spec,
        out_shape=jax.ShapeDtypeStruct(xs.shape, F32),
        compiler_params=_cparams(("arbitrary",), EXPERT_VMEM_LIMIT),
        name="experts",
    )(tile_blk, tile_exp, tile_valid, tile_first, xs, wgu, bg, bl, wd, bd, perm)


def _combine_kernel(pos_ref, next_pos_ref, h_ref, gt_ref, p_ref, wp_ref, gp_ref, wg_ref, gf_ref, ys_ref, o_ref,
                    rows_scr, sems, *, tm, n_steps):
    i = pl.program_id(0)
    slot = i % 2
    nc = h_ref.shape[1] // LANES

    def gather(idx_ref, s, start):
        def body(r, c):
            for k in range(TOP_K):
                cp = _row_copy(ys_ref, idx_ref[0, 0, k * tm + r], rows_scr.at[s, k], r, sems.at[s], nc)
                if start:
                    cp.start(priority=k % 2)
                else:
                    cp.wait()
            return c
        lax.fori_loop(0, tm, body, 0)

    @pl.when(i == 0)
    def _():
        gather(pos_ref, slot, True)

    @pl.when(i + 1 < n_steps)
    def _():
        gather(next_pos_ref, 1 - slot, True)

    emb = jnp.dot(p_ref[...].astype(BF16), wp_ref[...], preferred_element_type=F32)
    emb = emb * lax.rsqrt(jnp.mean(emb * emb, axis=-1, keepdims=True) + NORM_EPS) * gp_ref[...]

    gather(pos_ref, slot, False)

    gates = gt_ref[...]
    y = gates[:, 0:1] * _from_token_major(rows_scr.at[slot, 0], tm, nc)
    for k in range(1, TOP_K):
        y = y + gates[:, k:k + 1] * _from_token_major(rows_scr.at[slot, k], tm, nc)
    h = h_ref[...] + y
    gate = jax.nn.sigmoid(jnp.dot(h.astype(BF16), wg_ref[...], preferred_element_type=F32))
    h = h + gate * emb
    o_ref[...] = h * lax.rsqrt(jnp.mean(h * h, axis=-1, keepdims=True) + NORM_EPS) * gf_ref[...]


def _combine(pos_tiles, h1, gates_t, p2, wp, gp, wg, gf, ys):
    n, d = h1.shape
    tm = TOKEN_TILE
    row = lambda w: pl.BlockSpec((tm, w), lambda i: (i, 0))
    full = lambda a: pl.BlockSpec(a.shape, lambda i: (0,) * a.ndim)
    n_steps = n // tm
    kern = functools.partial(_combine_kernel, tm=tm, n_steps=n_steps)
    pos_spec = lambda off: pl.BlockSpec((1, 1, TOP_K * tm), lambda i: (jnp.minimum(i + off, n_steps - 1), 0, 0),
                                        memory_space=pltpu.SMEM)
    return pl.pallas_call(
        kern,
        grid=(n_steps,),
        in_specs=[pos_spec(0), pos_spec(1),
                  row(d), row(TOP_K), row(p2.shape[1]), full(wp), full(gp), full(wg), full(gf),
                  pl.BlockSpec(memory_space=pl.ANY)],
        out_specs=row(d),
        out_shape=jax.ShapeDtypeStruct((n, d), F32),
        scratch_shapes=[pltpu.VMEM((2, TOP_K, tm * (d // LANES), LANES), F32), pltpu.SemaphoreType.DMA((2,))],
        compiler_params=_cparams(("arbitrary",), VMEM_LIMIT),
        name="combine_ple",
    )(pos_tiles, pos_tiles, h1, gates_t, p2, wp, gp, wg, gf, ys)


def _t5_bucket(rel):
    n = jnp.maximum(rel, 0)
    max_exact = NUM_BUCKETS // 2
    nf = jnp.maximum(n, 1).astype(F32)
    large = max_exact + (jnp.log(nf / max_exact) / math.log(MAX_DISTANCE / max_exact)
                         * (NUM_BUCKETS - max_exact)).astype(I32)
    large = jnp.minimum(large, NUM_BUCKETS - 1)
    return jnp.where(n < max_exact, n, large)


def _bias_tiles(rel_bias, tq):
    n_heads = rel_bias.shape[1]
    lo = -tq
    rel = jnp.arange(lo, 3 * tq)
    vec = rel_bias.astype(F32)[_t5_bucket(rel)] * LOG2E
    vec = jnp.where((rel >= 0)[:, None], vec, -jnp.inf).T
    length = vec.shape[1]
    rev = vec[:, ::-1]
    tiles = []
    for d in range(3):
        c = d * tq - lo
        o = length - c - tq
        win = rev[:, o:o + 2 * tq]
        skew = jnp.tile(win, (1, tq))[:, :tq * (2 * tq - 1)].reshape(n_heads, tq, 2 * tq - 1)
        tiles.append(skew[:, :, tq - 1:])
    tiles.append(jnp.full((n_heads, tq, tq), -jnp.inf, F32))
    return jnp.stack(tiles, axis=1)


def _far_bias_is_constant(tq):
    n = np.arange(tq + 1, 4 * tq).astype(np.float32)
    max_exact = NUM_BUCKETS // 2
    large = max_exact + (np.log(n / max_exact) / math.log(MAX_DISTANCE / max_exact)
                         * (NUM_BUCKETS - max_exact)).astype(np.int32)
    return bool(np.all(np.minimum(large, NUM_BUCKETS - 1) == NUM_BUCKETS - 1))


def _tile_pos(pos, tm):
    k, n = pos.shape
    return pos.reshape(k, n // tm, tm).transpose(1, 0, 2).reshape(n // tm, 1, k * tm)


def _split_perm():
    src = np.arange(MXU_DIM)
    dst = np.where(src % 2 == 0, src // 2, LANES + src // 2)
    perm = np.zeros((MXU_DIM, MXU_DIM), np.float32)
    perm[src, dst] = 1.0
    return jnp.asarray(perm, BF16)


def kernel(x, p, w_in, w_out, attn_norm, moe_norm, rel_bias, lambda_q1, lambda_k1, lambda_q2, lambda_k2, subln,
           router_w, router_b, w_gate_up, b_gate_up, w_down, b_down, ple_proj, ple_norm, ple_gate, final_norm):
    batch, seq, d = x.shape
    assert w_in.shape[0] == 1
    n = batch * seq
    n_experts = router_w.shape[2]
    n_diff = (d // 2) // (2 * HEAD_DIM)
    n_sb = (d - d // 2) // HEAD_DIM
    qk_w = n_diff * HEAD_DIM
    sb0 = 4 * qk_w + n_diff * DIFF_V_DIM
    assert 2 * HEAD_DIM == LANES and (n_sb * HEAD_DIM) % SB_BLOCK == 0 and n_diff % 2 == 0
    assert sb0 % SB_BLOCK == 0 and ATTN_TILE & (ATTN_TILE - 1) == 0
    assert seq % ATTN_TILE == 0 and n % INPROJ_TILE == 0 and n % TOKEN_TILE == 0
    assert _far_bias_is_constant(ATTN_TILE)
    lambda_init = 0.8 - 0.6 * math.exp(-0.3 * 0)
    scale = HEAD_DIM ** -0.5

    x2 = x.reshape(n, d)
    col_scale = np.ones((w_in.shape[2],), np.float32)
    col_scale[:2 * qk_w] = scale * LOG2E
    col_scale[sb0:sb0 + n_sb * HEAD_DIM] = -scale * LOG2E
    w_all = (w_in[0] * col_scale[None, :]).astype(BF16)

    proj = _inproj(x2, attn_norm[0].reshape(1, d), w_all)

    o_diff = _diff_attention(
        proj, _bias_tiles(rel_bias, ATTN_TILE),
        lambda_q1[0].reshape(1, HEAD_DIM).astype(F32), lambda_k1[0].reshape(1, HEAD_DIM).astype(F32),
        lambda_q2[0].reshape(1, HEAD_DIM).astype(F32), lambda_k2[0].reshape(1, HEAD_DIM).astype(F32),
        subln[0].reshape(1, DIFF_V_DIM).astype(F32),
        batch=batch, seq=seq, n_heads=n_diff, lambda_init=lambda_init)

    t = ATTN_TILE
    u = (jnp.arange(t)[:, None] > jnp.arange(t)[None, :]).astype(BF16)
    o_sb = _sb_attention(proj, u, batch=batch, seq=seq, n_heads=n_sb, col0=sb0 // SB_BLOCK)

    wo = w_out[0].astype(BF16)
    rw = router_w[0].T
    rwh = rw.astype(BF16)
    rwl = (rw - rwh.astype(F32)).astype(BF16)
    tt = TOKEN_TILE
    ut = (jnp.arange(tt)[:, None] < jnp.arange(tt)[None, :]).astype(BF16)
    h1, xn, top_i, gates, rank, counts = _outproj_router(
        o_diff, o_sb, x2, wo[:n_diff * DIFF_V_DIM], wo[n_diff * DIFF_V_DIM:], moe_norm[0].reshape(1, d),
        rwh, rwl, router_b[0].reshape(n_experts, 1).astype(F32), ut)

    te = EXPERT_TILE
    n_rows = n * TOP_K + n_experts * te
    n_tiles = n_rows // te
    cnt = counts[:, 0]
    tiles_e = (cnt + te - 1) // te
    tile_end = jnp.cumsum(tiles_e)
    tile_start = tile_end - tiles_e
    base = tile_start * te
    n_valid = tile_end[-1]
    tidx = jnp.arange(n_tiles, dtype=I32)
    tile_valid = (tidx < n_valid).astype(I32)
    tile_blk = jnp.minimum(tidx, n_valid - 1).astype(I32)
    tile_exp = jnp.minimum(jnp.sum(tile_blk[:, None] >= tile_end[None, :], axis=1), n_experts - 1).astype(I32)
    tile_first = (jnp.any((tidx[:, None] == tile_start[None, :]) & (tiles_e[None, :] > 0), axis=1)
                  & (tidx < n_valid)).astype(I32)
    eids = jnp.arange(n_experts, dtype=I32)[:, None, None]
    pos = (rank + jnp.sum(jnp.where(top_i[None] == eids, base[:, None, None], 0), axis=0)).astype(I32)
    pos_tiles = _tile_pos(pos, tt)
    fill_lo = (base + cnt).astype(I32)
    fill_hi = (tile_end * te).astype(I32)

    xs = _dispatch(xn, pos_tiles, fill_lo, fill_hi, n_valid.reshape(1).astype(I32), n_rows)

    bgu = b_gate_up[0].astype(F32)
    ys = _experts(xs, tile_blk, tile_exp, tile_valid, tile_first, w_gate_up[0],
                  bgu[:, None, 0::2], bgu[:, None, 1::2], w_down[0], b_down[0][:, None, :].astype(F32),
                  _split_perm())

    out = _combine(pos_tiles, h1, gates.T, p[0].reshape(n, -1), ple_proj[0].astype(BF16),
                   ple_norm[0].reshape(1, d), ple_gate[0].astype(BF16), final_norm.reshape(1, d), ys)
    return out.reshape(batch, seq, d)
```

```python
import functools
import math

import numpy as np
import jax
import jax.numpy as jnp
from jax import lax
from jax.experimental import pallas as pl
from jax.experimental.pallas import tpu as pltpu

F32 = jnp.float32
BF16 = jnp.bfloat16
I32 = jnp.int32

HEAD_DIM = 64
DIFF_V_DIM = 2 * HEAD_DIM
NUM_BUCKETS = 32
MAX_DISTANCE = 128
TOP_K = 4
SWIGLU_LIMIT = 7.0
SWIGLU_ALPHA = 1.702
NORM_EPS = 1e-6
LOG2E = math.log2(math.e)

LANES = 128
SUBLANES = 8
MXU_DIM = 256
ATTN_TILE = 256
SB_BLOCK = 256
TOKEN_TILE = 256
INPROJ_TILE = 512
EXPERT_TILE = 512
ISSUE_UNROLL = 4
VMEM_LIMIT = 48 * 1024 * 1024
EXPERT_VMEM_LIMIT = 56 * 1024 * 1024


def _cparams(sem, vmem=None):
    return pltpu.CompilerParams(dimension_semantics=sem, vmem_limit_bytes=vmem)


def _inproj_kernel(x_ref, g_ref, w_ref, o_ref):
    x = x_ref[...]
    inv = lax.rsqrt(jnp.mean(x * x, axis=-1, keepdims=True) + NORM_EPS)
    hn = (x * inv * g_ref[...]).astype(BF16)
    o_ref[...] = jnp.dot(hn, w_ref[...], preferred_element_type=F32).astype(o_ref.dtype)


def _inproj(x2, g, w):
    n, d = x2.shape
    width = w.shape[1]
    tm = INPROJ_TILE
    return pl.pallas_call(
        _inproj_kernel,
        grid=(n // tm,),
        in_specs=[pl.BlockSpec((tm, d), lambda i: (i, 0)),
                  pl.BlockSpec((1, d), lambda i: (0, 0)),
                  pl.BlockSpec((d, width), lambda i: (0, 0))],
        out_specs=pl.BlockSpec((tm, width), lambda i: (i, 0)),
        out_shape=jax.ShapeDtypeStruct((n, width), BF16),
        compiler_params=_cparams(("arbitrary",), VMEM_LIMIT),
        name="inproj",
    )(x2, g, w)


_NT = (((1,), (1,)), ((), ()))


def _diff_attn_kernel(lq1_ref, lk1_ref, lq2_ref, lk2_ref, subln_ref, bias_ref, q1_ref, q2_ref, k1_ref, k2_ref,
                      v_ref, o_ref, s_scr, *, tq, lambda_init):
    qi = pl.program_id(2)
    lane = lax.broadcasted_iota(I32, (tq, LANES), 1)
    zq = jnp.zeros((tq, LANES), q1_ref.dtype)
    q1 = q1_ref[...]
    q2 = q2_ref[...]
    rows = []
    for hh in range(2):
        mine = (lane >= HEAD_DIM) == (hh == 1)
        rows.append(jnp.concatenate([jnp.where(mine, q1, zq), zq], axis=1))
        rows.append(jnp.concatenate([zq, jnp.where(mine, q2, zq)], axis=1))
    qq = jnp.concatenate(rows, axis=0)
    n_pairs = (qi + 2) // 2

    def key_tile(g, t):
        j = 2 * g + t
        d = qi - j
        return j, pl.multiple_of(j * tq, tq), jnp.where(d < 0, 3, jnp.minimum(d, 2))

    def scores(g, mx):
        for t in range(2):
            j, start, bi = key_tile(g, t)
            ba = bias_ref[0, bi]
            bb = bias_ref[1, bi]
            kk = jnp.concatenate([k1_ref[pl.ds(start, tq), :], k2_ref[pl.ds(start, tq), :]], axis=1)
            s = (lax.dot_general(qq, kk, _NT, preferred_element_type=F32)
                 + jnp.concatenate([ba, ba, bb, bb], axis=0))
            s_scr[j] = s
            mx = jnp.maximum(mx, jnp.maximum(s[:, :LANES], s[:, LANES:]))
        return mx

    mx = lax.fori_loop(0, n_pairs, scores, jnp.full((4 * tq, LANES), -jnp.inf, F32))
    m = jnp.max(mx, axis=1, keepdims=True)

    def weighted(g, carry):
        l_acc, acc = carry
        for t in range(2):
            j, start, _ = key_tile(g, t)
            p = jnp.exp2(s_scr[j] - m)
            l_acc = l_acc + (p[:, :LANES] + p[:, LANES:])
            acc = acc + jnp.dot(p.astype(BF16), v_ref[pl.ds(start, tq), :], preferred_element_type=F32)
        return l_acc, acc

    l_acc, acc = lax.fori_loop(0, n_pairs, weighted,
                               (jnp.zeros((4 * tq, LANES), F32), jnp.zeros((4 * tq, 2 * LANES), F32)))
    l = jnp.sum(l_acc, axis=1, keepdims=True)
    lam = (jnp.exp(jnp.sum(lq1_ref[...] * lk1_ref[...], axis=1, keepdims=True))
           - jnp.exp(jnp.sum(lq2_ref[...] * lk2_ref[...], axis=1, keepdims=True)) + lambda_init)
    outs = []
    for hh in range(2):
        r1, r2 = 2 * hh * tq, (2 * hh + 1) * tq
        a1 = acc[r1:r1 + tq, hh * LANES:(hh + 1) * LANES]
        a2 = acc[r2:r2 + tq, hh * LANES:(hh + 1) * LANES]
        o = a1 / l[r1:r1 + tq] - lam * (a2 / l[r2:r2 + tq])
        o = o * lax.rsqrt(jnp.mean(o * o, axis=-1, keepdims=True) + NORM_EPS) * subln_ref[...]
        outs.append((o * (1.0 - lambda_init)).astype(o_ref.dtype))
    o_ref[...] = jnp.concatenate(outs, axis=1)


def _diff_attention(proj, bias_tiles, lq1, lk1, lq2, lk2, subln, *, batch, seq, n_heads, lambda_init):
    n = proj.shape[0]
    tq = ATTN_TILE
    nq = seq // tq
    assert nq % 2 == 0
    nb = n_heads // 2
    small = lambda shape: pl.BlockSpec(shape, lambda b, h, i: (0,) * len(shape))
    kern = functools.partial(_diff_attn_kernel, tq=tq, lambda_init=lambda_init)
    qspec = lambda grp: pl.BlockSpec((tq, LANES), lambda b, h, i: (b * nq + i, grp * nb + h))
    kspec = lambda grp: pl.BlockSpec((seq, LANES), lambda b, h, i: (b, grp * nb + h))
    return pl.pallas_call(
        kern,
        grid=(batch, nb, nq),
        in_specs=[small((1, HEAD_DIM)), small((1, HEAD_DIM)), small((1, HEAD_DIM)), small((1, HEAD_DIM)),
                  small((1, DIFF_V_DIM)),
                  pl.BlockSpec((2, 4, tq, tq), lambda b, h, i: (h, 0, 0, 0)),
                  qspec(0), qspec(1), kspec(2), kspec(3),
                  pl.BlockSpec((seq, 2 * LANES), lambda b, h, i: (b, 2 * nb + h))],
        out_specs=pl.BlockSpec((tq, 2 * LANES), lambda b, h, i: (b * nq + i, h)),
        out_shape=jax.ShapeDtypeStruct((n, n_heads * DIFF_V_DIM), BF16),
        scratch_shapes=[pltpu.VMEM((nq, 4 * tq, tq), F32)],
        compiler_params=_cparams(("arbitrary", "arbitrary", "arbitrary"), VMEM_LIMIT),
        name="diff_attention",
    )(lq1, lk1, lq2, lk2, subln, bias_tiles, proj, proj, proj, proj, proj)


def _sb_attn_kernel(u_ref, q_ref, k_ref, v_ref, o_ref, *, tq):
    qi = pl.program_id(2)
    q = q_ref[...]
    n_heads = q.shape[1] // HEAD_DIM
    lane = lax.broadcasted_iota(I32, q.shape, 1)
    zero = jnp.zeros_like(q)
    qs = jnp.concatenate([jnp.where((lane >= hh * HEAD_DIM) & (lane < (hh + 1) * HEAD_DIM), q, zero)
                          for hh in range(n_heads)], axis=0)
    row = lax.broadcasted_iota(I32, (n_heads * tq, tq), 0)
    col = lax.broadcasted_iota(I32, (n_heads * tq, tq), 1)
    strict = col < (row & (tq - 1))
    u = u_ref[...]

    def tile(j, carry, acc, diag, valid):
        start = pl.multiple_of(j * tq, tq)
        kj = k_ref[pl.ds(start, tq), :]
        vj = v_ref[pl.ds(start, tq), :]
        n = lax.dot_general(qs, kj, _NT, preferred_element_type=F32)
        lk = jnp.minimum(n, 0.0) - jnp.log2(1.0 + jnp.exp2(-jnp.abs(n)))
        if diag:
            lk = jnp.where(strict, lk, 0.0)
        sfx = jnp.dot(lk.astype(BF16), u, preferred_element_type=F32)
        w = jnp.exp2(lk - n + (sfx + carry))
        if diag:
            w = jnp.where(strict, w, 0.0)
        if valid is not None:
            w = jnp.where(valid, w, 0.0)
        acc = acc + jnp.dot(w.astype(BF16), vj, preferred_element_type=F32)
        carry = carry + (sfx[:, :1] + lk[:, :1])
        return carry, acc

    def group(g, state, diag):
        ja = qi - 2 * g
        jb = ja - 1
        carry, acc = tile(ja, state[0], state[1], diag, None)
        return tile(jnp.maximum(jb, 0), carry, acc, False, jb >= 0)

    state = group(0, (jnp.zeros((n_heads * tq, 1), F32), jnp.zeros((n_heads * tq, q.shape[1]), F32)), True)
    acc = lax.fori_loop(1, (qi + 2) // 2, lambda g, st: group(g, st, False), state)[1]
    out = acc[:tq]
    for hh in range(1, n_heads):
        out = jnp.where(lane >= hh * HEAD_DIM, acc[hh * tq:(hh + 1) * tq], out)
    o_ref[...] = out.astype(o_ref.dtype)


def _sb_attention(proj, u, *, batch, seq, n_heads, col0):
    n = proj.shape[0]
    tq = ATTN_TILE
    nq = seq // tq
    nblk = n_heads * HEAD_DIM // SB_BLOCK
    kern = functools.partial(_sb_attn_kernel, tq=tq)
    return pl.pallas_call(
        kern,
        grid=(batch, nblk, nq),
        in_specs=[pl.BlockSpec((tq, tq), lambda b, p, i: (0, 0)),
                  pl.BlockSpec((tq, SB_BLOCK), lambda b, p, i: (b * nq + i, col0 + p)),
                  pl.BlockSpec((seq, SB_BLOCK), lambda b, p, i: (b, col0 + nblk + p)),
                  pl.BlockSpec((seq, SB_BLOCK), lambda b, p, i: (b, col0 + 2 * nblk + p))],
        out_specs=pl.BlockSpec((tq, SB_BLOCK), lambda b, p, i: (b * nq + i, p)),
        out_shape=jax.ShapeDtypeStruct((n, n_heads * HEAD_DIM), BF16),
        compiler_params=_cparams(("arbitrary", "arbitrary", "arbitrary"), VMEM_LIMIT),
        name="sb_attention",
    )(u, proj, proj, proj)


def _outproj_router_kernel(od_ref, os_ref, x_ref, wod_ref, wos_ref, g_ref, rwh_ref, rwl_ref, rb_ref, ut_ref,
                           h_ref, xn_ref, ti_ref, gt_ref, rk_ref, cnt_ref, cnt_scr, *, n_experts):
    @pl.when(pl.program_id(0) == 0)
    def _():
        cnt_scr[...] = jnp.zeros_like(cnt_scr)

    attn = (jnp.dot(od_ref[...], wod_ref[...], preferred_element_type=F32)
            + jnp.dot(os_ref[...], wos_ref[...], preferred_element_type=F32))
    h = x_ref[...] + attn
    h_ref[...] = h
    xn = h * lax.rsqrt(jnp.mean(h * h, axis=-1, keepdims=True) + NORM_EPS) * g_ref[...]
    xn_ref[...] = xn
    xh = xn.astype(BF16)
    xl = (xn - xh.astype(F32)).astype(BF16)
    rwh = rwh_ref[...]
    logits = (lax.dot_general(rwh, xh, _NT, preferred_element_type=F32)
              + lax.dot_general(rwh, xl, _NT, preferred_element_type=F32)
              + lax.dot_general(rwl_ref[...], xh, _NT, preferred_element_type=F32)
              + rb_ref[...])
    eidx = lax.broadcasted_iota(I32, logits.shape, 0)
    work = logits
    vals, idxs, sels = [], [], []
    for _ in range(TOP_K):
        mx = jnp.max(work, axis=0, keepdims=True)
        ix = jnp.min(jnp.where(work == mx, eidx, n_experts), axis=0, keepdims=True)
        sel = eidx == ix
        work = jnp.where(sel, -jnp.inf, work)
        vals.append(mx)
        idxs.append(ix)
        sels.append(sel)
    ex = [jnp.exp(v - vals[0]) for v in vals]
    den = ex[0] + ex[1] + ex[2] + ex[3]
    ti_ref[...] = jnp.concatenate(idxs, axis=0)
    gt_ref[...] = jnp.concatenate([e / den for e in ex], axis=0)
    onehot = jnp.where(sels[0] | sels[1] | sels[2] | sels[3], 1.0, 0.0)
    rank = jnp.dot(onehot.astype(BF16), ut_ref[...], preferred_element_type=F32) + cnt_scr[...]
    rk_ref[...] = jnp.concatenate(
        [jnp.sum(jnp.where(s, rank, 0.0), axis=0, keepdims=True) for s in sels], axis=0).astype(I32)
    cnt_scr[...] += jnp.sum(onehot, axis=1, keepdims=True)
    cnt_ref[...] = cnt_scr[...].astype(I32)


def _outproj_router(o_diff, o_sb, x2, wod, wos, g, rwh, rwl, rb, ut):
    n, d = x2.shape
    e = rwh.shape[0]
    tm = TOKEN_TILE
    row = lambda w: pl.BlockSpec((tm, w), lambda i: (i, 0))
    full = lambda a: pl.BlockSpec(a.shape, lambda i: (0,) * a.ndim)
    kcol = pl.BlockSpec((TOP_K, tm), lambda i: (0, i))
    kern = functools.partial(_outproj_router_kernel, n_experts=e)
    return pl.pallas_call(
        kern,
        grid=(n // tm,),
        in_specs=[row(o_diff.shape[1]), row(o_sb.shape[1]), row(d), full(wod), full(wos), full(g),
                  full(rwh), full(rwl), full(rb), full(ut)],
        out_specs=[row(d), row(d), kcol, kcol, kcol, pl.BlockSpec((e, 1), lambda i: (0, 0))],
        out_shape=[jax.ShapeDtypeStruct((n, d), F32), jax.ShapeDtypeStruct((n, d), F32),
                   jax.ShapeDtypeStruct((TOP_K, n), I32), jax.ShapeDtypeStruct((TOP_K, n), F32),
                   jax.ShapeDtypeStruct((TOP_K, n), I32), jax.ShapeDtypeStruct((e, 1), I32)],
        scratch_shapes=[pltpu.VMEM((e, 1), F32)],
        compiler_params=_cparams(("arbitrary",), VMEM_LIMIT),
        name="outproj_router",
    )(o_diff, o_sb, x2, wod, wos, g, rwh, rwl, rb, ut)


def _to_token_major(dst_ref, x, n_tok):
    nc = x.shape[1] // LANES
    for c in range(nc):
        dst_ref[pl.ds(c, n_tok, stride=nc), :] = x[:, c * LANES:(c + 1) * LANES]


def _from_token_major(src_ref, n_tok, nc):
    return jnp.concatenate([src_ref[pl.ds(c, n_tok, stride=nc), :] for c in range(nc)], axis=1)


def _row_copy(src_ref, src_row, dst_ref, dst_row, sem, nc):
    return pltpu.make_async_copy(src_ref.at[pl.ds(pl.multiple_of(src_row * nc, nc), nc), :],
                                 dst_ref.at[pl.ds(pl.multiple_of(dst_row * nc, nc), nc), :], sem)


def _dispatch_kernel(fill_lo_ref, fill_hi_ref, nvalid_ref, pos_ref, xn_ref, xs_ref, tok_scr, zero_scr, sems,
                     *, tm, te, nc, n_experts, n_tiles):
    sem, tail_sem, fill_sem = sems.at[0], sems.at[1], sems.at[2]
    _to_token_major(tok_scr, xn_ref[...], tm)

    def issue(g, c):
        for u in range(ISSUE_UNROLL):
            r = g * ISSUE_UNROLL + u
            for k in range(TOP_K):
                _row_copy(tok_scr, r, xs_ref, pos_ref[0, 0, k * tm + r], sem, nc).start(priority=k % 2)
        return c

    lax.fori_loop(0, tm // ISSUE_UNROLL, issue, 0)

    @pl.when(pl.program_id(0) == 0)
    def _():
        zero_scr[...] = jnp.zeros_like(zero_scr)

        def tail_copy(t):
            return pltpu.make_async_copy(
                zero_scr, xs_ref.at[pl.ds(pl.multiple_of(t * te * nc, te * nc), te * nc), :], tail_sem)

        def tail(t, c):
            tail_copy(t).start()
            return c

        lax.fori_loop(nvalid_ref[0], n_tiles, tail, 0)

        def drain_tail(t, c):
            tail_copy(t).wait()
            return c

        lax.fori_loop(nvalid_ref[0], n_tiles, drain_tail, 0)
        for e in range(n_experts):
            def fill(r, c):
                _row_copy(zero_scr, 0, xs_ref, r, fill_sem, nc).start()
                return c
            lax.fori_loop(fill_lo_ref[e], fill_hi_ref[e], fill, 0)
        for e in range(n_experts):
            def drain_fill(r, c):
                _row_copy(zero_scr, 0, xs_ref, r, fill_sem, nc).wait()
                return c
            lax.fori_loop(fill_lo_ref[e], fill_hi_ref[e], drain_fill, 0)

    for k in range(TOP_K):
        pltpu.make_async_copy(tok_scr, xs_ref.at[pl.ds(0, tm * nc), :], sem).wait()


def _dispatch(xn, pos_tiles, fill_lo, fill_hi, n_valid, n_rows):
    n, d = xn.shape
    tm = TOKEN_TILE
    te = EXPERT_TILE
    nc = d // LANES
    assert nc % SUBLANES == 0
    e = fill_lo.shape[0]
    kern = functools.partial(_dispatch_kernel, tm=tm, te=te, nc=nc, n_experts=e, n_tiles=n_rows // te)
    grid_spec = pltpu.PrefetchScalarGridSpec(
        num_scalar_prefetch=3,
        grid=(n // tm,),
        in_specs=[pl.BlockSpec((1, 1, TOP_K * tm), lambda i, lo, hi, nv: (i, 0, 0), memory_space=pltpu.SMEM),
                  pl.BlockSpec((tm, d), lambda i, lo, hi, nv: (i, 0))],
        out_specs=pl.BlockSpec(memory_space=pl.ANY),
        scratch_shapes=[pltpu.VMEM((tm * nc, LANES), F32), pltpu.VMEM((te * nc, LANES), F32),
                        pltpu.SemaphoreType.DMA((3,))],
    )
    return pl.pallas_call(
        kern,
        grid_spec=grid_spec,
        out_shape=jax.ShapeDtypeStruct((n_rows * nc, LANES), F32),
        compiler_params=_cparams(("arbitrary",)),
        name="dispatch",
    )(fill_lo, fill_hi, n_valid, pos_tiles, xn)


def _expert_kernel(blk_ref, exp_ref, valid_ref, first_ref, xs_ref, wgu_ref, bg_ref, bl_ref, wd_ref, bd_ref,
                   perm_ref, o_ref, wg_scr, wl_scr, wd_scr, *, te):
    i = pl.program_id(0)
    nc = wd_ref.shape[2] // LANES
    valid = valid_ref[i] == 1

    @pl.when(first_ref[i] == 1)
    def _():
        perm = perm_ref[...]
        for g in range(wgu_ref.shape[2] // MXU_DIM):
            blk = wgu_ref[0, :, g * MXU_DIM:(g + 1) * MXU_DIM].astype(BF16)
            sp = jnp.dot(blk, perm, preferred_element_type=F32).astype(BF16)
            wg_scr[:, g * LANES:(g + 1) * LANES] = sp[:, :LANES]
            wl_scr[:, g * LANES:(g + 1) * LANES] = sp[:, LANES:]
        wd_scr[...] = wd_ref[0].astype(BF16)

    @pl.when(jnp.logical_not(valid))
    def _():
        o_ref[...] = jnp.zeros_like(o_ref)

    @pl.when(valid)
    def _():
        x = _from_token_major(xs_ref, te, nc).astype(BF16)
        glu = jnp.dot(x, wg_scr[...], preferred_element_type=F32) + bg_ref[0]
        lin = jnp.dot(x, wl_scr[...], preferred_element_type=F32) + bl_ref[0]
        glu = jnp.minimum(glu, SWIGLU_LIMIT)
        lin = jnp.clip(lin, -SWIGLU_LIMIT, SWIGLU_LIMIT)
        act = glu * jax.nn.sigmoid(SWIGLU_ALPHA * glu) * (lin + 1.0)
        out = jnp.dot(act.astype(BF16), wd_scr[...], preferred_element_type=F32) + bd_ref[0]
        _to_token_major(o_ref, out, te)


def _experts(xs, tile_blk, tile_exp, tile_valid, tile_first, wgu, bg, bl, wd, bd, perm):
    de, d = wd.shape[1], wd.shape[2]
    nc = d // LANES
    te = EXPERT_TILE
    n_tiles = xs.shape[0] // (te * nc)
    per_e = lambda shape: pl.BlockSpec((1,) + shape, lambda i, blk, ex, va, fi: (ex[i], 0, 0))
    grid_spec = pltpu.PrefetchScalarGridSpec(
        num_scalar_prefetch=4,
        grid=(n_tiles,),
        in_specs=[pl.BlockSpec((te * nc, LANES), lambda i, blk, ex, va, fi: (blk[i], 0)),
                  per_e((d, 2 * de)), per_e((1, de)), per_e((1, de)), per_e((de, d)), per_e((1, d)),
                  pl.BlockSpec((MXU_DIM, MXU_DIM), lambda i, blk, ex, va, fi: (0, 0))],
        out_specs=pl.BlockSpec((te * nc, LANES), lambda i, blk, ex, va, fi: (i, 0)),
        scratch_shapes=[pltpu.VMEM((d, de), BF16), pltpu.VMEM((d, de), BF16), pltpu.VMEM((de, d), BF16)],
    )
    return pl.pallas_call(
        functools.partial(_expert_kernel, te=te),
        grid_spec=grid_spec,
        out_shape=jax.ShapeDtypeStruct(xs.shape, F32),
        compiler_params=_cparams(("arbitrary",), EXPERT_VMEM_LIMIT),
        name="experts",
    )(tile_blk, tile_exp, tile_valid, tile_first, xs, wgu, bg, bl, wd, bd, perm)


def _combine_kernel(pos_ref, next_pos_ref, h_ref, gt_ref, p_ref, wp_ref, gp_ref, wg_ref, gf_ref, ys_ref, o_ref,
                    rows_scr, sems, *, tm, n_steps):
    i = pl.program_id(0)
    slot = i % 2
    nc = h_ref.shape[1] // LANES

    def start_gather(idx_ref, s):
        def body(g, c):
            for u in range(ISSUE_UNROLL):
                r = g * ISSUE_UNROLL + u
                for k in range(TOP_K):
                    _row_copy(ys_ref, idx_ref[0, 0, k * tm + r], rows_scr.at[s, k], r, sems.at[s],
                              nc).start(priority=k % 2)
            return c
        lax.fori_loop(0, tm // ISSUE_UNROLL, body, 0)

    @pl.when(i == 0)
    def _():
        start_gather(pos_ref, slot)

    @pl.when(i + 1 < n_steps)
    def _():
        start_gather(next_pos_ref, 1 - slot)

    emb = jnp.dot(p_ref[...].astype(BF16), wp_ref[...], preferred_element_type=F32)
    emb = emb * lax.rsqrt(jnp.mean(emb * emb, axis=-1, keepdims=True) + NORM_EPS) * gp_ref[...]

    for k in range(TOP_K):
        pltpu.make_async_copy(ys_ref.at[pl.ds(0, tm * nc), :], rows_scr.at[slot, k], sems.at[slot]).wait()

    gates = gt_ref[...]
    y = gates[:, 0:1] * _from_token_major(rows_scr.at[slot, 0], tm, nc)
    for k in range(1, TOP_K):
        y = y + gates[:, k:k + 1] * _from_token_major(rows_scr.at[slot, k], tm, nc)
    h = h_ref[...] + y
    gate = jax.nn.sigmoid(jnp.dot(h.astype(BF16), wg_ref[...], preferred_element_type=F32))
    h = h + gate * emb
    o_ref[...] = h * lax.rsqrt(jnp.mean(h * h, axis=-1, keepdims=True) + NORM_EPS) * gf_ref[...]


def _combine(pos_tiles, h1, gates_t, p2, wp, gp, wg, gf, ys):
    n, d = h1.shape
    tm = TOKEN_TILE
    row = lambda w: pl.BlockSpec((tm, w), lambda i: (i, 0))
    full = lambda a: pl.BlockSpec(a.shape, lambda i: (0,) * a.ndim)
    n_steps = n // tm
    kern = functools.partial(_combine_kernel, tm=tm, n_steps=n_steps)
    pos_spec = lambda off: pl.BlockSpec((1, 1, TOP_K * tm), lambda i: (jnp.minimum(i + off, n_steps - 1), 0, 0),
                                        memory_space=pltpu.SMEM)
    return pl.pallas_call(
        kern,
        grid=(n_steps,),
        in_specs=[pos_spec(0), pos_spec(1),
                  row(d), row(TOP_K), row(p2.shape[1]), full(wp), full(gp), full(wg), full(gf),
                  pl.BlockSpec(memory_space=pl.ANY)],
        out_specs=row(d),
        out_shape=jax.ShapeDtypeStruct((n, d), F32),
        scratch_shapes=[pltpu.VMEM((2, TOP_K, tm * (d // LANES), LANES), F32), pltpu.SemaphoreType.DMA((2,))],
        compiler_params=_cparams(("arbitrary",), VMEM_LIMIT),
        name="combine_ple",
    )(pos_tiles, pos_tiles, h1, gates_t, p2, wp, gp, wg, gf, ys)


def _t5_bucket(rel):
    n = jnp.maximum(rel, 0)
    max_exact = NUM_BUCKETS // 2
    nf = jnp.maximum(n, 1).astype(F32)
    large = max_exact + (jnp.log(nf / max_exact) / math.log(MAX_DISTANCE / max_exact)
                         * (NUM_BUCKETS - max_exact)).astype(I32)
    large = jnp.minimum(large, NUM_BUCKETS - 1)
    return jnp.where(n < max_exact, n, large)


def _bias_tiles(rel_bias, tq):
    n_heads = rel_bias.shape[1]
    lo = -tq
    rel = jnp.arange(lo, 3 * tq)
    vec = rel_bias.astype(F32)[_t5_bucket(rel)] * LOG2E
    vec = jnp.where((rel >= 0)[:, None], vec, -jnp.inf).T
    length = vec.shape[1]
    rev = vec[:, ::-1]
    tiles = []
    for d in range(3):
        c = d * tq - lo
        o = length - c - tq
        win = rev[:, o:o + 2 * tq]
        skew = jnp.tile(win, (1, tq))[:, :tq * (2 * tq - 1)].reshape(n_heads, tq, 2 * tq - 1)
        tiles.append(skew[:, :, tq - 1:])
    tiles.append(jnp.full((n_heads, tq, tq), -jnp.inf, F32))
    return jnp.stack(tiles, axis=1)


def _far_bias_is_constant(tq):
    n = np.arange(tq + 1, 4 * tq).astype(np.float32)
    max_exact = NUM_BUCKETS // 2
    large = max_exact + (np.log(n / max_exact) / math.log(MAX_DISTANCE / max_exact)
                         * (NUM_BUCKETS - max_exact)).astype(np.int32)
    return bool(np.all(np.minimum(large, NUM_BUCKETS - 1) == NUM_BUCKETS - 1))


def _tile_pos(pos, tm):
    k, n = pos.shape
    return pos.reshape(k, n // tm, tm).transpose(1, 0, 2).reshape(n // tm, 1, k * tm)


def _split_perm():
    src = np.arange(MXU_DIM)
    dst = np.where(src % 2 == 0, src // 2, LANES + src // 2)
    perm = np.zeros((MXU_DIM, MXU_DIM), np.float32)
    perm[src, dst] = 1.0
    return jnp.asarray(perm, BF16)


def kernel(x, p, w_in, w_out, attn_norm, moe_norm, rel_bias, lambda_q1, lambda_k1, lambda_q2, lambda_k2, subln,
           router_w, router_b, w_gate_up, b_gate_up, w_down, b_down, ple_proj, ple_norm, ple_gate, final_norm):
    batch, seq, d = x.shape
    assert w_in.shape[0] == 1
    n = batch * seq
    n_experts = router_w.shape[2]
    n_diff = (d // 2) // (2 * HEAD_DIM)
    n_sb = (d - d // 2) // HEAD_DIM
    qk_w = n_diff * HEAD_DIM
    sb0 = 4 * qk_w + n_diff * DIFF_V_DIM
    assert 2 * HEAD_DIM == LANES and (n_sb * HEAD_DIM) % SB_BLOCK == 0 and n_diff % 2 == 0
    assert sb0 % SB_BLOCK == 0 and ATTN_TILE & (ATTN_TILE - 1) == 0
    assert seq % ATTN_TILE == 0 and n % INPROJ_TILE == 0 and n % TOKEN_TILE == 0
    assert _far_bias_is_constant(ATTN_TILE)
    lambda_init = 0.8 - 0.6 * math.exp(-0.3 * 0)
    scale = HEAD_DIM ** -0.5

    x2 = x.reshape(n, d)
    col_scale = np.ones((w_in.shape[2],), np.float32)
    col_scale[:2 * qk_w] = scale * LOG2E
    col_scale[sb0:sb0 + n_sb * HEAD_DIM] = -scale * LOG2E
    w_all = (w_in[0] * col_scale[None, :]).astype(BF16)

    proj = _inproj(x2, attn_norm[0].reshape(1, d), w_all)

    o_diff = _diff_attention(
        proj, _bias_tiles(rel_bias, ATTN_TILE),
        lambda_q1[0].reshape(1, HEAD_DIM).astype(F32), lambda_k1[0].reshape(1, HEAD_DIM).astype(F32),
        lambda_q2[0].reshape(1, HEAD_DIM).astype(F32), lambda_k2[0].reshape(1, HEAD_DIM).astype(F32),
        subln[0].reshape(1, DIFF_V_DIM).astype(F32),
        batch=batch, seq=seq, n_heads=n_diff, lambda_init=lambda_init)

    t = ATTN_TILE
    u = (jnp.arange(t)[:, None] > jnp.arange(t)[None, :]).astype(BF16)
    o_sb = _sb_attention(proj, u, batch=batch, seq=seq, n_heads=n_sb, col0=sb0 // SB_BLOCK)

    wo = w_out[0].astype(BF16)
    rw = router_w[0].T
    rwh = rw.astype(BF16)
    rwl = (rw - rwh.astype(F32)).astype(BF16)
    tt = TOKEN_TILE
    ut = (jnp.arange(tt)[:, None] < jnp.arange(tt)[None, :]).astype(BF16)
    h1, xn, top_i, gates, rank, counts = _outproj_router(
        o_diff, o_sb, x2, wo[:n_diff * DIFF_V_DIM], wo[n_diff * DIFF_V_DIM:], moe_norm[0].reshape(1, d),
        rwh, rwl, router_b[0].reshape(n_experts, 1).astype(F32), ut)

    te = EXPERT_TILE
    n_rows = n * TOP_K + n_experts * te
    n_tiles = n_rows // te
    cnt = counts[:, 0]
    tiles_e = (cnt + te - 1) // te
    tile_end = jnp.cumsum(tiles_e)
    tile_start = tile_end - tiles_e
    base = tile_start * te
    n_valid = tile_end[-1]
    tidx = jnp.arange(n_tiles, dtype=I32)
    tile_valid = (tidx < n_valid).astype(I32)
    tile_blk = jnp.minimum(tidx, n_valid - 1).astype(I32)
    tile_exp = jnp.minimum(jnp.sum(tile_blk[:, None] >= tile_end[None, :], axis=1), n_experts - 1).astype(I32)
    tile_first = (jnp.any((tidx[:, None] == tile_start[None, :]) & (tiles_e[None, :] > 0), axis=1)
                  & (tidx < n_valid)).astype(I32)
    eids = jnp.arange(n_experts, dtype=I32)[:, None, None]
    pos = (rank + jnp.sum(jnp.where(top_i[None] == eids, base[:, None, None], 0), axis=0)).astype(I32)
    pos_tiles = _tile_pos(pos, tt)
    fill_lo = (base + cnt).astype(I32)
    fill_hi = (tile_end * te).astype(I32)

    xs = _dispatch(xn, pos_tiles, fill_lo, fill_hi, n_valid.reshape(1).astype(I32), n_rows)

    bgu = b_gate_up[0].astype(F32)
    ys = _experts(xs, tile_blk, tile_exp, tile_valid, tile_first, w_gate_up[0],
                  bgu[:, None, 0::2], bgu[:, None, 1::2], w_down[0], b_down[0][:, None, :].astype(F32),
                  _split_perm())

    out = _combine(pos_tiles, h1, gates.T, p[0].reshape(n, -1), ple_proj[0].astype(BF16),
                   ple_norm[0].reshape(1, d), ple_gate[0].astype(BF16), final_norm.reshape(1, d), ys)
    return out.reshape(batch, seq, d)
```

```python
import functools
import math

import numpy as np
import jax
import jax.numpy as jnp
from jax import lax
from jax.experimental import pallas as pl
from jax.experimental.pallas import tpu as pltpu

F32 = jnp.float32
BF16 = jnp.bfloat16
I32 = jnp.int32

HEAD_DIM = 64
DIFF_V_DIM = 2 * HEAD_DIM
NUM_BUCKETS = 32
MAX_DISTANCE = 128
TOP_K = 4
SWIGLU_LIMIT = 7.0
SWIGLU_ALPHA = 1.702
NORM_EPS = 1e-6
LOG2E = math.log2(math.e)

LANES = 128
SUBLANES = 8
MXU_DIM = 256
ATTN_TILE = 256
SB_BLOCK = 256
TOKEN_TILE = 256
INPROJ_TILE = 512
EXPERT_TILE = 512
ISSUE_UNROLL = 4
VMEM_LIMIT = 48 * 1024 * 1024
EXPERT_VMEM_LIMIT = 56 * 1024 * 1024


def _cparams(sem, vmem=None):
    return pltpu.CompilerParams(dimension_semantics=sem, vmem_limit_bytes=vmem)


def _inproj_kernel(x_ref, g_ref, w_ref, o_ref):
    x = x_ref[...]
    inv = lax.rsqrt(jnp.mean(x * x, axis=-1, keepdims=True) + NORM_EPS)
    hn = (x * inv * g_ref[...]).astype(BF16)
    o_ref[...] = jnp.dot(hn, w_ref[...], preferred_element_type=F32).astype(o_ref.dtype)


def _inproj(x2, g, w):
    n, d = x2.shape
    width = w.shape[1]
    tm = INPROJ_TILE
    return pl.pallas_call(
        _inproj_kernel,
        grid=(n // tm,),
        in_specs=[pl.BlockSpec((tm, d), lambda i: (i, 0)),
                  pl.BlockSpec((1, d), lambda i: (0, 0)),
                  pl.BlockSpec((d, width), lambda i: (0, 0))],
        out_specs=pl.BlockSpec((tm, width), lambda i: (i, 0)),
        out_shape=jax.ShapeDtypeStruct((n, width), BF16),
        compiler_params=_cparams(("arbitrary",), VMEM_LIMIT),
        name="inproj",
    )(x2, g, w)


_NT = (((1,), (1,)), ((), ()))


def _diff_attn_kernel(lq1_ref, lk1_ref, lq2_ref, lk2_ref, subln_ref, bias_ref, q1_ref, q2_ref, k1_ref, k2_ref,
                      v_ref, o_ref, s_scr, *, tq, lambda_init):
    qi = pl.program_id(2)
    lane = lax.broadcasted_iota(I32, (tq, LANES), 1)
    zq = jnp.zeros((tq, LANES), q1_ref.dtype)
    q1 = q1_ref[...]
    q2 = q2_ref[...]
    rows = []
    for hh in range(2):
        mine = (lane >= HEAD_DIM) == (hh == 1)
        rows.append(jnp.concatenate([jnp.where(mine, q1, zq), zq], axis=1))
        rows.append(jnp.concatenate([zq, jnp.where(mine, q2, zq)], axis=1))
    qq = jnp.concatenate(rows, axis=0)
    n_pairs = (qi + 2) // 2

    def key_tile(g, t):
        j = 2 * g + t
        d = qi - j
        return j, pl.multiple_of(j * tq, tq), jnp.where(d < 0, 3, jnp.minimum(d, 2))

    def scores(g, mx):
        for t in range(2):
            j, start, bi = key_tile(g, t)
            ba = bias_ref[0, bi]
            bb = bias_ref[1, bi]
            kk = jnp.concatenate([k1_ref[pl.ds(start, tq), :], k2_ref[pl.ds(start, tq), :]], axis=1)
            s = (lax.dot_general(qq, kk, _NT, preferred_element_type=F32)
                 + jnp.concatenate([ba, ba, bb, bb], axis=0))
            s_scr[j] = s
            mx = jnp.maximum(mx, jnp.maximum(s[:, :LANES], s[:, LANES:]))
        return mx

    mx = lax.fori_loop(0, n_pairs, scores, jnp.full((4 * tq, LANES), -jnp.inf, F32))
    m = jnp.max(mx, axis=1, keepdims=True)

    def weighted(g, carry):
        l_acc, acc = carry
        for t in range(2):
            j, start, _ = key_tile(g, t)
            p = jnp.exp2(s_scr[j] - m)
            l_acc = l_acc + (p[:, :LANES] + p[:, LANES:])
            acc = acc + jnp.dot(p.astype(BF16), v_ref[pl.ds(start, tq), :], preferred_element_type=F32)
        return l_acc, acc

    l_acc, acc = lax.fori_loop(0, n_pairs, weighted,
                               (jnp.zeros((4 * tq, LANES), F32), jnp.zeros((4 * tq, 2 * LANES), F32)))
    l = jnp.sum(l_acc, axis=1, keepdims=True)
    lam = (jnp.exp(jnp.sum(lq1_ref[...] * lk1_ref[...], axis=1, keepdims=True))
           - jnp.exp(jnp.sum(lq2_ref[...] * lk2_ref[...], axis=1, keepdims=True)) + lambda_init)
    outs = []
    for hh in range(2):
        r1, r2 = 2 * hh * tq, (2 * hh + 1) * tq
        a1 = acc[r1:r1 + tq, hh * LANES:(hh + 1) * LANES]
        a2 = acc[r2:r2 + tq, hh * LANES:(hh + 1) * LANES]
        o = a1 / l[r1:r1 + tq] - lam * (a2 / l[r2:r2 + tq])
        o = o * lax.rsqrt(jnp.mean(o * o, axis=-1, keepdims=True) + NORM_EPS) * subln_ref[...]
        outs.append((o * (1.0 - lambda_init)).astype(o_ref.dtype))
    o_ref[...] = jnp.concatenate(outs, axis=1)


def _diff_attention(proj, bias_tiles, lq1, lk1, lq2, lk2, subln, *, batch, seq, n_heads, lambda_init):
    n = proj.shape[0]
    tq = ATTN_TILE
    nq = seq // tq
    assert nq % 2 == 0
    nb = n_heads // 2
    small = lambda shape: pl.BlockSpec(shape, lambda b, h, i: (0,) * len(shape))
    kern = functools.partial(_diff_attn_kernel, tq=tq, lambda_init=lambda_init)
    qspec = lambda grp: pl.BlockSpec((tq, LANES), lambda b, h, i: (b * nq + i, grp * nb + h))
    kspec = lambda grp: pl.BlockSpec((seq, LANES), lambda b, h, i: (b, grp * nb + h))
    return pl.pallas_call(
        kern,
        grid=(batch, nb, nq),
        in_specs=[small((1, HEAD_DIM)), small((1, HEAD_DIM)), small((1, HEAD_DIM)), small((1, HEAD_DIM)),
                  small((1, DIFF_V_DIM)),
                  pl.BlockSpec((2, 4, tq, tq), lambda b, h, i: (h, 0, 0, 0)),
                  qspec(0), qspec(1), kspec(2), kspec(3),
                  pl.BlockSpec((seq, 2 * LANES), lambda b, h, i: (b, 2 * nb + h))],
        out_specs=pl.BlockSpec((tq, 2 * LANES), lambda b, h, i: (b * nq + i, h)),
        out_shape=jax.ShapeDtypeStruct((n, n_heads * DIFF_V_DIM), BF16),
        scratch_shapes=[pltpu.VMEM((nq, 4 * tq, tq), F32)],
        compiler_params=_cparams(("arbitrary", "arbitrary", "arbitrary"), VMEM_LIMIT),
        name="diff_attention",
    )(lq1, lk1, lq2, lk2, subln, bias_tiles, proj, proj, proj, proj, proj)


def _sb_attn_kernel(u_ref, q_ref, k_ref, v_ref, o_ref, *, tq):
    qi = pl.program_id(2)
    q = q_ref[...]
    n_heads = q.shape[1] // HEAD_DIM
    lane = lax.broadcasted_iota(I32, q.shape, 1)
    zero = jnp.zeros_like(q)
    qs = jnp.concatenate([jnp.where((lane >= hh * HEAD_DIM) & (lane < (hh + 1) * HEAD_DIM), q, zero)
                          for hh in range(n_heads)], axis=0)
    row = lax.broadcasted_iota(I32, (n_heads * tq, tq), 0)
    col = lax.broadcasted_iota(I32, (n_heads * tq, tq), 1)
    strict = col < (row & (tq - 1))
    u = u_ref[...]

    def tile(j, carry, acc, diag, valid):
        start = pl.multiple_of(j * tq, tq)
        kj = k_ref[pl.ds(start, tq), :]
        vj = v_ref[pl.ds(start, tq), :]
        n = lax.dot_general(qs, kj, _NT, preferred_element_type=F32)
        lk = jnp.minimum(n, 0.0) - jnp.log2(1.0 + jnp.exp2(-jnp.abs(n)))
        if diag:
            lk = jnp.where(strict, lk, 0.0)
        sfx = jnp.dot(lk.astype(BF16), u, preferred_element_type=F32)
        w = jnp.exp2(lk - n + (sfx + carry))
        if diag:
            w = jnp.where(strict, w, 0.0)
        if valid is not None:
            w = jnp.where(valid, w, 0.0)
        acc = acc + jnp.dot(w.astype(BF16), vj, preferred_element_type=F32)
        carry = carry + (sfx[:, :1] + lk[:, :1])
        return carry, acc

    def group(g, state, diag):
        ja = qi - 2 * g
        jb = ja - 1
        carry, acc = tile(ja, state[0], state[1], diag, None)
        return tile(jnp.maximum(jb, 0), carry, acc, False, jb >= 0)

    state = group(0, (jnp.zeros((n_heads * tq, 1), F32), jnp.zeros((n_heads * tq, q.shape[1]), F32)), True)
    acc = lax.fori_loop(1, (qi + 2) // 2, lambda g, st: group(g, st, False), state)[1]
    out = acc[:tq]
    for hh in range(1, n_heads):
        out = jnp.where(lane >= hh * HEAD_DIM, acc[hh * tq:(hh + 1) * tq], out)
    o_ref[...] = out.astype(o_ref.dtype)


def _sb_attention(proj, u, *, batch, seq, n_heads, col0):
    n = proj.shape[0]
    tq = ATTN_TILE
    nq = seq // tq
    nblk = n_heads * HEAD_DIM // SB_BLOCK
    kern = functools.partial(_sb_attn_kernel, tq=tq)
    return pl.pallas_call(
        kern,
        grid=(batch, nblk, nq),
        in_specs=[pl.BlockSpec((tq, tq), lambda b, p, i: (0, 0)),
                  pl.BlockSpec((tq, SB_BLOCK), lambda b, p, i: (b * nq + i, col0 + p)),
                  pl.BlockSpec((seq, SB_BLOCK), lambda b, p, i: (b, col0 + nblk + p)),
                  pl.BlockSpec((seq, SB_BLOCK), lambda b, p, i: (b, col0 + 2 * nblk + p))],
        out_specs=pl.BlockSpec((tq, SB_BLOCK), lambda b, p, i: (b * nq + i, p)),
        out_shape=jax.ShapeDtypeStruct((n, n_heads * HEAD_DIM), BF16),
        compiler_params=_cparams(("arbitrary", "arbitrary", "arbitrary"), VMEM_LIMIT),
        name="sb_attention",
    )(u, proj, proj, proj)


def _attn_kernel(lq1_ref, lk1_ref, lq2_ref, lk2_ref, subln_ref, bias_ref, u_ref,
                 dq1_ref, dq2_ref, dk1_ref, dk2_ref, dv_ref, sq_ref, sk_ref, sv_ref,
                 od_ref, os_ref, s_scr, t_scr, rt_scr, *, tq, nq, lambda_init):
    qi = pl.program_id(2)
    n_pairs = (qi + 2) // 2
    lane = lax.broadcasted_iota(I32, (tq, LANES), 1)
    zq = jnp.zeros((tq, LANES), dq1_ref.dtype)
    dq1 = dq1_ref[...]
    dq2 = dq2_ref[...]
    rows = []
    for hh in range(2):
        mine = (lane >= HEAD_DIM) == (hh == 1)
        rows.append(jnp.concatenate([jnp.where(mine, dq1, zq), zq], axis=1))
        rows.append(jnp.concatenate([zq, jnp.where(mine, dq2, zq)], axis=1))
    qq = jnp.concatenate(rows, axis=0)

    sq = sq_ref[...]
    n_sb = sq.shape[1] // HEAD_DIM
    lane_s = lax.broadcasted_iota(I32, sq.shape, 1)
    zs = jnp.zeros_like(sq)
    qs = jnp.concatenate([jnp.where((lane_s >= hh * HEAD_DIM) & (lane_s < (hh + 1) * HEAD_DIM), sq, zs)
                          for hh in range(n_sb)], axis=0)
    row = lax.broadcasted_iota(I32, (n_sb * tq, tq), 0)
    col = lax.broadcasted_iota(I32, (n_sb * tq, tq), 1)
    strict = col < (row & (tq - 1))
    u = u_ref[...]

    def pair(g):
        ja = qi - 2 * g
        jb = ja - 1
        valid_b = jb >= 0
        return ((ja, ja, jnp.minimum(2 * g, 2), None),
                (jnp.maximum(jb, 0), jnp.where(valid_b, jb, nq), jnp.where(valid_b, jnp.minimum(2 * g + 1, 2), 3),
                 valid_b))

    def pass1(g, mx, diag):
        for t, (j, slot, bi, _) in enumerate(pair(g)):
            start = pl.multiple_of(j * tq, tq)
            n = lax.dot_general(qs, sk_ref[pl.ds(start, tq), :], _NT, preferred_element_type=F32)
            lk = jnp.minimum(n, 0.0) - jnp.log2(1.0 + jnp.exp2(-jnp.abs(n)))
            if diag and t == 0:
                lk = jnp.where(strict, lk, 0.0)
            sfx = jnp.dot(lk.astype(BF16), u, preferred_element_type=F32)
            t_scr[slot] = (lk - n) + sfx
            rt_scr[slot] = sfx[:, :1] + lk[:, :1]
            ba = bias_ref[0, bi]
            bb = bias_ref[1, bi]
            kk = jnp.concatenate([dk1_ref[pl.ds(start, tq), :], dk2_ref[pl.ds(start, tq), :]], axis=1)
            s = (lax.dot_general(qq, kk, _NT, preferred_element_type=F32)
                 + jnp.concatenate([ba, ba, bb, bb], axis=0))
            s_scr[slot] = s
            mx = jnp.maximum(mx, jnp.maximum(s[:, :LANES], s[:, LANES:]))
        return mx

    mx = pass1(0, jnp.full((4 * tq, LANES), -jnp.inf, F32), True)
    mx = lax.fori_loop(1, n_pairs, lambda g, c: pass1(g, c, False), mx)
    m = jnp.max(mx, axis=1, keepdims=True)

    def pass2(g, state, diag):
        carry, acc_s, l_acc, acc_d = state
        for t, (j, slot, _, valid) in enumerate(pair(g)):
            start = pl.multiple_of(j * tq, tq)
            w = jnp.exp2(t_scr[slot] + carry)
            if diag and t == 0:
                w = jnp.where(strict, w, 0.0)
            if valid is not None:
                w = jnp.where(valid, w, 0.0)
            acc_s = acc_s + jnp.dot(w.astype(BF16), sv_ref[pl.ds(start, tq), :], preferred_element_type=F32)
            carry = carry + rt_scr[slot]
            p = jnp.exp2(s_scr[slot] - m)
            l_acc = l_acc + (p[:, :LANES] + p[:, LANES:])
            acc_d = acc_d + jnp.dot(p.astype(BF16), dv_ref[pl.ds(start, tq), :], preferred_element_type=F32)
        return carry, acc_s, l_acc, acc_d

    state = (jnp.zeros((n_sb * tq, 1), F32), jnp.zeros((n_sb * tq, sq.shape[1]), F32),
             jnp.zeros((4 * tq, LANES), F32), jnp.zeros((4 * tq, 2 * LANES), F32))
    state = pass2(0, state, True)
    _, acc_s, l_acc, acc_d = lax.fori_loop(1, n_pairs, lambda g, st: pass2(g, st, False), state)

    out = acc_s[:tq]
    for hh in range(1, n_sb):
        out = jnp.where(lane_s >= hh * HEAD_DIM, acc_s[hh * tq:(hh + 1) * tq], out)
    os_ref[...] = out.astype(os_ref.dtype)

    l = jnp.sum(l_acc, axis=1, keepdims=True)
    lam = (jnp.exp(jnp.sum(lq1_ref[...] * lk1_ref[...], axis=1, keepdims=True))
           - jnp.exp(jnp.sum(lq2_ref[...] * lk2_ref[...], axis=1, keepdims=True)) + lambda_init)
    outs = []
    for hh in range(2):
        r1, r2 = 2 * hh * tq, (2 * hh + 1) * tq
        a1 = acc_d[r1:r1 + tq, hh * LANES:(hh + 1) * LANES]
        a2 = acc_d[r2:r2 + tq, hh * LANES:(hh + 1) * LANES]
        o = a1 / l[r1:r1 + tq] - lam * (a2 / l[r2:r2 + tq])
        o = o * lax.rsqrt(jnp.mean(o * o, axis=-1, keepdims=True) + NORM_EPS) * subln_ref[...]
        outs.append((o * (1.0 - lambda_init)).astype(od_ref.dtype))
    od_ref[...] = jnp.concatenate(outs, axis=1)


def _attention(proj, bias_tiles, u, lq1, lk1, lq2, lk2, subln, *, batch, seq, n_diff, n_sb, sb_col0, lambda_init):
    n = proj.shape[0]
    tq = ATTN_TILE
    nq = seq // tq
    nb = n_diff // 2
    nblk = n_sb * HEAD_DIM // SB_BLOCK
    assert nq % 2 == 0 and nb == nblk
    small = lambda shape: pl.BlockSpec(shape, lambda b, h, i: (0,) * len(shape))
    qspec = lambda grp: pl.BlockSpec((tq, LANES), lambda b, h, i: (b * nq + i, grp * nb + h))
    kspec = lambda grp: pl.BlockSpec((seq, LANES), lambda b, h, i: (b, grp * nb + h))
    sspec = lambda rows, grp: pl.BlockSpec(
        (rows, SB_BLOCK), lambda b, h, i: ((b * nq + i) if rows == tq else b, sb_col0 + grp * nblk + h))
    kern = functools.partial(_attn_kernel, tq=tq, nq=nq, lambda_init=lambda_init)
    return pl.pallas_call(
        kern,
        grid=(batch, nb, nq),
        in_specs=[small((1, HEAD_DIM)), small((1, HEAD_DIM)), small((1, HEAD_DIM)), small((1, HEAD_DIM)),
                  small((1, DIFF_V_DIM)),
                  pl.BlockSpec((2, 4, tq, tq), lambda b, h, i: (h, 0, 0, 0)),
                  pl.BlockSpec((tq, tq), lambda b, h, i: (0, 0)),
                  qspec(0), qspec(1), kspec(2), kspec(3),
                  pl.BlockSpec((seq, 2 * LANES), lambda b, h, i: (b, 2 * nb + h)),
                  sspec(tq, 0), sspec(seq, 1), sspec(seq, 2)],
        out_specs=[pl.BlockSpec((tq, 2 * LANES), lambda b, h, i: (b * nq + i, h)),
                   pl.BlockSpec((tq, SB_BLOCK), lambda b, h, i: (b * nq + i, h))],
        out_shape=[jax.ShapeDtypeStruct((n, n_diff * DIFF_V_DIM), BF16),
                   jax.ShapeDtypeStruct((n, n_sb * HEAD_DIM), BF16)],
        scratch_shapes=[pltpu.VMEM((nq + 1, 4 * tq, tq), F32),
                        pltpu.VMEM((nq + 1, (SB_BLOCK // HEAD_DIM) * tq, tq), F32),
                        pltpu.VMEM((nq + 1, (SB_BLOCK // HEAD_DIM) * tq, 1), F32)],
        compiler_params=_cparams(("arbitrary", "arbitrary", "arbitrary"), EXPERT_VMEM_LIMIT),
        name="attention",
    )(lq1, lk1, lq2, lk2, subln, bias_tiles, u, proj, proj, proj, proj, proj, proj, proj, proj)


def _outproj_router_kernel(od_ref, os_ref, x_ref, wod_ref, wos_ref, g_ref, rwh_ref, rwl_ref, rb_ref, ut_ref,
                           h_ref, xn_ref, ti_ref, gt_ref, rk_ref, cnt_ref, cnt_scr, *, n_experts):
    @pl.when(pl.program_id(0) == 0)
    def _():
        cnt_scr[...] = jnp.zeros_like(cnt_scr)

    attn = (jnp.dot(od_ref[...], wod_ref[...], preferred_element_type=F32)
            + jnp.dot(os_ref[...], wos_ref[...], preferred_element_type=F32))
    h = x_ref[...] + attn
    h_ref[...] = h
    xn = h * lax.rsqrt(jnp.mean(h * h, axis=-1, keepdims=True) + NORM_EPS) * g_ref[...]
    xn_ref[...] = xn
    xh = xn.astype(BF16)
    xl = (xn - xh.astype(F32)).astype(BF16)
    rwh = rwh_ref[...]
    logits = (lax.dot_general(rwh, xh, _NT, preferred_element_type=F32)
              + lax.dot_general(rwh, xl, _NT, preferred_element_type=F32)
              + lax.dot_general(rwl_ref[...], xh, _NT, preferred_element_type=F32)
              + rb_ref[...])
    eidx = lax.broadcasted_iota(I32, logits.shape, 0)
    work = logits
    vals, idxs, sels = [], [], []
    for _ in range(TOP_K):
        mx = jnp.max(work, axis=0, keepdims=True)
        ix = jnp.min(jnp.where(work == mx, eidx, n_experts), axis=0, keepdims=True)
        sel = eidx == ix
        work = jnp.where(sel, -jnp.inf, work)
        vals.append(mx)
        idxs.append(ix)
        sels.append(sel)
    ex = [jnp.exp(v - vals[0]) for v in vals]
    den = ex[0] + ex[1] + ex[2] + ex[3]
    ti_ref[...] = jnp.concatenate(idxs, axis=0)
    gt_ref[...] = jnp.concatenate([e / den for e in ex], axis=0)
    onehot = jnp.where(sels[0] | sels[1] | sels[2] | sels[3], 1.0, 0.0)
    rank = jnp.dot(onehot.astype(BF16), ut_ref[...], preferred_element_type=F32) + cnt_scr[...]
    rk_ref[...] = jnp.concatenate(
        [jnp.sum(jnp.where(s, rank, 0.0), axis=0, keepdims=True) for s in sels], axis=0).astype(I32)
    cnt_scr[...] += jnp.sum(onehot, axis=1, keepdims=True)
    cnt_ref[...] = cnt_scr[...].astype(I32)


def _outproj_router(o_diff, o_sb, x2, wod, wos, g, rwh, rwl, rb, ut):
    n, d = x2.shape
    e = rwh.shape[0]
    tm = TOKEN_TILE
    row = lambda w: pl.BlockSpec((tm, w), lambda i: (i, 0))
    full = lambda a: pl.BlockSpec(a.shape, lambda i: (0,) * a.ndim)
    kcol = pl.BlockSpec((TOP_K, tm), lambda i: (0, i))
    kern = functools.partial(_outproj_router_kernel, n_experts=e)
    return pl.pallas_call(
        kern,
        grid=(n // tm,),
        in_specs=[row(o_diff.shape[1]), row(o_sb.shape[1]), row(d), full(wod), full(wos), full(g),
                  full(rwh), full(rwl), full(rb), full(ut)],
        out_specs=[row(d), row(d), kcol, kcol, kcol, pl.BlockSpec((e, 1), lambda i: (0, 0))],
        out_shape=[jax.ShapeDtypeStruct((n, d), F32), jax.ShapeDtypeStruct((n, d), F32),
                   jax.ShapeDtypeStruct((TOP_K, n), I32), jax.ShapeDtypeStruct((TOP_K, n), F32),
                   jax.ShapeDtypeStruct((TOP_K, n), I32), jax.ShapeDtypeStruct((e, 1), I32)],
        scratch_shapes=[pltpu.VMEM((e, 1), F32)],
        compiler_params=_cparams(("arbitrary",), VMEM_LIMIT),
        name="outproj_router",
    )(o_diff, o_sb, x2, wod, wos, g, rwh, rwl, rb, ut)


def _to_token_major(dst_ref, x, n_tok):
    nc = x.shape[1] // LANES
    for c in range(nc):
        dst_ref[pl.ds(c, n_tok, stride=nc), :] = x[:, c * LANES:(c + 1) * LANES]


def _from_token_major(src_ref, n_tok, nc):
    return jnp.concatenate([src_ref[pl.ds(c, n_tok, stride=nc), :] for c in range(nc)], axis=1)


def _row_copy(src_ref, src_row, dst_ref, dst_row, sem, nc):
    return pltpu.make_async_copy(src_ref.at[pl.ds(pl.multiple_of(src_row * nc, nc), nc), :],
                                 dst_ref.at[pl.ds(pl.multiple_of(dst_row * nc, nc), nc), :], sem)


def _dispatch_kernel(fill_lo_ref, fill_hi_ref, nvalid_ref, pos_ref, xn_ref, xs_ref, tok_scr, zero_scr, sems,
                     *, tm, te, nc, n_experts, n_tiles):
    sem, tail_sem, fill_sem = sems.at[0], sems.at[1], sems.at[2]
    _to_token_major(tok_scr, xn_ref[...], tm)

    def issue(g, c):
        for u in range(ISSUE_UNROLL):
            r = g * ISSUE_UNROLL + u
            for k in range(TOP_K):
                _row_copy(tok_scr, r, xs_ref, pos_ref[0, 0, k * tm + r], sem, nc).start(priority=k % 2)
        return c

    lax.fori_loop(0, tm // ISSUE_UNROLL, issue, 0)

    @pl.when(pl.program_id(0) == 0)
    def _():
        zero_scr[...] = jnp.zeros_like(zero_scr)

        def tail_copy(t):
            return pltpu.make_async_copy(
                zero_scr, xs_ref.at[pl.ds(pl.multiple_of(t * te * nc, te * nc), te * nc), :], tail_sem)

        def tail(t, c):
            tail_copy(t).start()
            return c

        lax.fori_loop(nvalid_ref[0], n_tiles, tail, 0)

        def drain_tail(t, c):
            tail_copy(t).wait()
            return c

        lax.fori_loop(nvalid_ref[0], n_tiles, drain_tail, 0)
        for e in range(n_experts):
            def fill(r, c):
                _row_copy(zero_scr, 0, xs_ref, r, fill_sem, nc).start()
                return c
            lax.fori_loop(fill_lo_ref[e], fill_hi_ref[e], fill, 0)
        for e in range(n_experts):
            def drain_fill(r, c):
                _row_copy(zero_scr, 0, xs_ref, r, fill_sem, nc).wait()
                return c
            lax.fori_loop(fill_lo_ref[e], fill_hi_ref[e], drain_fill, 0)

    for k in range(TOP_K):
        pltpu.make_async_copy(tok_scr, xs_ref.at[pl.ds(0, tm * nc), :], sem).wait()


def _dispatch(xn, pos_tiles, fill_lo, fill_hi, n_valid, n_rows):
    n, d = xn.shape
    tm = TOKEN_TILE
    te = EXPERT_TILE
    nc = d // LANES
    assert nc % SUBLANES == 0
    e = fill_lo.shape[0]
    kern = functools.partial(_dispatch_kernel, tm=tm, te=te, nc=nc, n_experts=e, n_tiles=n_rows // te)
    grid_spec = pltpu.PrefetchScalarGridSpec(
        num_scalar_prefetch=3,
        grid=(n // tm,),
        in_specs=[pl.BlockSpec((1, 1, TOP_K * tm), lambda i, lo, hi, nv: (i, 0, 0), memory_space=pltpu.SMEM),
                  pl.BlockSpec((tm, d), lambda i, lo, hi, nv: (i, 0))],
        out_specs=pl.BlockSpec(memory_space=pl.ANY),
        scratch_shapes=[pltpu.VMEM((tm * nc, LANES), F32), pltpu.VMEM((te * nc, LANES), F32),
                        pltpu.SemaphoreType.DMA((3,))],
    )
    return pl.pallas_call(
        kern,
        grid_spec=grid_spec,
        out_shape=jax.ShapeDtypeStruct((n_rows * nc, LANES), F32),
        compiler_params=_cparams(("arbitrary",)),
        name="dispatch",
    )(fill_lo, fill_hi, n_valid, pos_tiles, xn)


def _expert_kernel(blk_ref, exp_ref, valid_ref, first_ref, xs_ref, wgu_ref, bg_ref, bl_ref, wd_ref, bd_ref,
                   perm_ref, o_ref, wg_scr, wl_scr, wd_scr, *, te):
    i = pl.program_id(0)
    nc = wd_ref.shape[2] // LANES
    valid = valid_ref[i] == 1

    @pl.when(first_ref[i] == 1)
    def _():
        perm = perm_ref[...]
        for g in range(wgu_ref.shape[2] // MXU_DIM):
            blk = wgu_ref[0, :, g * MXU_DIM:(g + 1) * MXU_DIM].astype(BF16)
            sp = jnp.dot(blk, perm, preferred_element_type=F32).astype(BF16)
            wg_scr[:, g * LANES:(g + 1) * LANES] = sp[:, :LANES]
            wl_scr[:, g * LANES:(g + 1) * LANES] = sp[:, LANES:]
        wd_scr[...] = wd_ref[0].astype(BF16)

    @pl.when(jnp.logical_not(valid))
    def _():
        o_ref[...] = jnp.zeros_like(o_ref)

    @pl.when(valid)
    def _():
        x = _from_token_major(xs_ref, te, nc).astype(BF16)
        glu = jnp.dot(x, wg_scr[...], preferred_element_type=F32) + bg_ref[0]
        lin = jnp.dot(x, wl_scr[...], preferred_element_type=F32) + bl_ref[0]
        glu = jnp.minimum(glu, SWIGLU_LIMIT)
        lin = jnp.clip(lin, -SWIGLU_LIMIT, SWIGLU_LIMIT)
        act = glu * jax.nn.sigmoid(SWIGLU_ALPHA * glu) * (lin + 1.0)
        out = jnp.dot(act.astype(BF16), wd_scr[...], preferred_element_type=F32) + bd_ref[0]
        _to_token_major(o_ref, out, te)


def _experts(xs, tile_blk, tile_exp, tile_valid, tile_first, wgu, bg, bl, wd, bd, perm):
    de, d = wd.shape[1], wd.shape[2]
    nc = d // LANES
    te = EXPERT_TILE
    n_tiles = xs.shape[0] // (te * nc)
    per_e = lambda shape: pl.BlockSpec((1,) + shape, lambda i, blk, ex, va, fi: (ex[i], 0, 0))
    grid_spec = pltpu.PrefetchScalarGridSpec(
        num_scalar_prefetch=4,
        grid=(n_tiles,),
        in_specs=[pl.BlockSpec((te * nc, LANES), lambda i, blk, ex, va, fi: (blk[i], 0)),
                  per_e((d, 2 * de)), per_e((1, de)), per_e((1, de)), per_e((de, d)), per_e((1, d)),
                  pl.BlockSpec((MXU_DIM, MXU_DIM), lambda i, blk, ex, va, fi: (0, 0))],
        out_specs=pl.BlockSpec((te * nc, LANES), lambda i, blk, ex, va, fi: (i, 0)),
        scratch_shapes=[pltpu.VMEM((d, de), BF16), pltpu.VMEM((d, de), BF16), pltpu.VMEM((de, d), BF16)],
    )
    return pl.pallas_call(
        functools.partial(_expert_kernel, te=te),
        grid_spec=grid_spec,
        out_shape=jax.ShapeDtypeStruct(xs.shape, F32),
        compiler_params=_cparams(("arbitrary",), EXPERT_VMEM_LIMIT),
        name="experts",
    )(tile_blk, tile_exp, tile_valid, tile_first, xs, wgu, bg, bl, wd, bd, perm)


def _combine_kernel(pos_ref, next_pos_ref, h_ref, gt_ref, p_ref, wp_ref, gp_ref, wg_ref, gf_ref, ys_ref, o_ref,
                    rows_scr, sems, *, tm, n_steps):
    i = pl.program_id(0)
    slot = i % 2
    nc = h_ref.shape[1] // LANES

    def start_gather(idx_ref, s):
        def body(g, c):
            for u in range(ISSUE_UNROLL):
                r = g * ISSUE_UNROLL + u
                for k in range(TOP_K):
                    _row_copy(ys_ref, idx_ref[0, 0, k * tm + r], rows_scr.at[s, k], r, sems.at[s],
                              nc).start(priority=k % 2)
            return c
        lax.fori_loop(0, tm // ISSUE_UNROLL, body, 0)

    @pl.when(i == 0)
    def _():
        start_gather(pos_ref, slot)

    @pl.when(i + 1 < n_steps)
    def _():
        start_gather(next_pos_ref, 1 - slot)

    emb = jnp.dot(p_ref[...].astype(BF16), wp_ref[...], preferred_element_type=F32)
    emb = emb * lax.rsqrt(jnp.mean(emb * emb, axis=-1, keepdims=True) + NORM_EPS) * gp_ref[...]

    for k in range(TOP_K):
        pltpu.make_async_copy(ys_ref.at[pl.ds(0, tm * nc), :], rows_scr.at[slot, k], sems.at[slot]).wait()

    gates = gt_ref[...]
    y = gates[:, 0:1] * _from_token_major(rows_scr.at[slot, 0], tm, nc)
    for k in range(1, TOP_K):
        y = y + gates[:, k:k + 1] * _from_token_major(rows_scr.at[slot, k], tm, nc)
    h = h_ref[...] + y
    gate = jax.nn.sigmoid(jnp.dot(h.astype(BF16), wg_ref[...], preferred_element_type=F32))
    h = h + gate * emb
    o_ref[...] = h * lax.rsqrt(jnp.mean(h * h, axis=-1, keepdims=True) + NORM_EPS) * gf_ref[...]


def _combine(pos_tiles, h1, gates_t, p2, wp, gp, wg, gf, ys):
    n, d = h1.shape
    tm = TOKEN_TILE
    row = lambda w: pl.BlockSpec((tm, w), lambda i: (i, 0))
    full = lambda a: pl.BlockSpec(a.shape, lambda i: (0,) * a.ndim)
    n_steps = n // tm
    kern = functools.partial(_combine_kernel, tm=tm, n_steps=n_steps)
    pos_spec = lambda off: pl.BlockSpec((1, 1, TOP_K * tm), lambda i: (jnp.minimum(i + off, n_steps - 1), 0, 0),
                                        memory_space=pltpu.SMEM)
    return pl.pallas_call(
        kern,
        grid=(n_steps,),
        in_specs=[pos_spec(0), pos_spec(1),
                  row(d), row(TOP_K), row(p2.shape[1]), full(wp), full(gp), full(wg), full(gf),
                  pl.BlockSpec(memory_space=pl.ANY)],
        out_specs=row(d),
        out_shape=jax.ShapeDtypeStruct((n, d), F32),
        scratch_shapes=[pltpu.VMEM((2, TOP_K, tm * (d // LANES), LANES), F32), pltpu.SemaphoreType.DMA((2,))],
        compiler_params=_cparams(("arbitrary",), VMEM_LIMIT),
        name="combine_ple",
    )(pos_tiles, pos_tiles, h1, gates_t, p2, wp, gp, wg, gf, ys)


def _t5_bucket(rel):
    n = jnp.maximum(rel, 0)
    max_exact = NUM_BUCKETS // 2
    nf = jnp.maximum(n, 1).astype(F32)
    large = max_exact + (jnp.log(nf / max_exact) / math.log(MAX_DISTANCE / max_exact)
                         * (NUM_BUCKETS - max_exact)).astype(I32)
    large = jnp.minimum(large, NUM_BUCKETS - 1)
    return jnp.where(n < max_exact, n, large)


def _bias_tiles(rel_bias, tq):
    n_heads = rel_bias.shape[1]
    lo = -tq
    rel = jnp.arange(lo, 3 * tq)
    vec = rel_bias.astype(F32)[_t5_bucket(rel)] * LOG2E
    vec = jnp.where((rel >= 0)[:, None], vec, -jnp.inf).T
    length = vec.shape[1]
    rev = vec[:, ::-1]
    tiles = []
    for d in range(3):
        c = d * tq - lo
        o = length - c - tq
        win = rev[:, o:o + 2 * tq]
        skew = jnp.tile(win, (1, tq))[:, :tq * (2 * tq - 1)].reshape(n_heads, tq, 2 * tq - 1)
        tiles.append(skew[:, :, tq - 1:])
    tiles.append(jnp.full((n_heads, tq, tq), -jnp.inf, F32))
    return jnp.stack(tiles, axis=1)


def _far_bias_is_constant(tq):
    n = np.arange(tq + 1, 4 * tq).astype(np.float32)
    max_exact = NUM_BUCKETS // 2
    large = max_exact + (np.log(n / max_exact) / math.log(MAX_DISTANCE / max_exact)
                         * (NUM_BUCKETS - max_exact)).astype(np.int32)
    return bool(np.all(np.minimum(large, NUM_BUCKETS - 1) == NUM_BUCKETS - 1))


def _tile_pos(pos, tm):
    k, n = pos.shape
    return pos.reshape(k, n // tm, tm).transpose(1, 0, 2).reshape(n // tm, 1, k * tm)


def _split_perm():
    src = np.arange(MXU_DIM)
    dst = np.where(src % 2 == 0, src // 2, LANES + src // 2)
    perm = np.zeros((MXU_DIM, MXU_DIM), np.float32)
    perm[src, dst] = 1.0
    return jnp.asarray(perm, BF16)


def kernel(x, p, w_in, w_out, attn_norm, moe_norm, rel_bias, lambda_q1, lambda_k1, lambda_q2, lambda_k2, subln,
           router_w, router_b, w_gate_up, b_gate_up, w_down, b_down, ple_proj, ple_norm, ple_gate, final_norm):
    batch, seq, d = x.shape
    assert w_in.shape[0] == 1
    n = batch * seq
    n_experts = router_w.shape[2]
    n_diff = (d // 2) // (2 * HEAD_DIM)
    n_sb = (d - d // 2) // HEAD_DIM
    qk_w = n_diff * HEAD_DIM
    sb0 = 4 * qk_w + n_diff * DIFF_V_DIM
    assert 2 * HEAD_DIM == LANES and (n_sb * HEAD_DIM) % SB_BLOCK == 0 and n_diff % 2 == 0
    assert sb0 % SB_BLOCK == 0 and ATTN_TILE & (ATTN_TILE - 1) == 0
    assert seq % ATTN_TILE == 0 and n % INPROJ_TILE == 0 and n % TOKEN_TILE == 0
    assert _far_bias_is_constant(ATTN_TILE)
    lambda_init = 0.8 - 0.6 * math.exp(-0.3 * 0)
    scale = HEAD_DIM ** -0.5

    x2 = x.reshape(n, d)
    col_scale = np.ones((w_in.shape[2],), np.float32)
    col_scale[:2 * qk_w] = scale * LOG2E
    col_scale[sb0:sb0 + n_sb * HEAD_DIM] = -scale * LOG2E
    w_all = (w_in[0] * col_scale[None, :]).astype(BF16)

    proj = _inproj(x2, attn_norm[0].reshape(1, d), w_all)

    t = ATTN_TILE
    u = (jnp.arange(t)[:, None] > jnp.arange(t)[None, :]).astype(BF16)
    o_diff, o_sb = _attention(
        proj, _bias_tiles(rel_bias, ATTN_TILE), u,
        lambda_q1[0].reshape(1, HEAD_DIM).astype(F32), lambda_k1[0].reshape(1, HEAD_DIM).astype(F32),
        lambda_q2[0].reshape(1, HEAD_DIM).astype(F32), lambda_k2[0].reshape(1, HEAD_DIM).astype(F32),
        subln[0].reshape(1, DIFF_V_DIM).astype(F32),
        batch=batch, seq=seq, n_diff=n_diff, n_sb=n_sb, sb_col0=sb0 // SB_BLOCK, lambda_init=lambda_init)

    wo = w_out[0].astype(BF16)
    rw = router_w[0].T
    rwh = rw.astype(BF16)
    rwl = (rw - rwh.astype(F32)).astype(BF16)
    tt = TOKEN_TILE
    ut = (jnp.arange(tt)[:, None] < jnp.arange(tt)[None, :]).astype(BF16)
    h1, xn, top_i, gates, rank, counts = _outproj_router(
        o_diff, o_sb, x2, wo[:n_diff * DIFF_V_DIM], wo[n_diff * DIFF_V_DIM:], moe_norm[0].reshape(1, d),
        rwh, rwl, router_b[0].reshape(n_experts, 1).astype(F32), ut)

    te = EXPERT_TILE
    n_rows = n * TOP_K + n_experts * te
    n_tiles = n_rows // te
    cnt = counts[:, 0]
    tiles_e = (cnt + te - 1) // te
    tile_end = jnp.cumsum(tiles_e)
    tile_start = tile_end - tiles_e
    base = tile_start * te
    n_valid = tile_end[-1]
    tidx = jnp.arange(n_tiles, dtype=I32)
    tile_valid = (tidx < n_valid).astype(I32)
    tile_blk = jnp.minimum(tidx, n_valid - 1).astype(I32)
    tile_exp = jnp.minimum(jnp.sum(tile_blk[:, None] >= tile_end[None, :], axis=1), n_experts - 1).astype(I32)
    tile_first = (jnp.any((tidx[:, None] == tile_start[None, :]) & (tiles_e[None, :] > 0), axis=1)
                  & (tidx < n_valid)).astype(I32)
    eids = jnp.arange(n_experts, dtype=I32)[:, None, None]
    pos = (rank + jnp.sum(jnp.where(top_i[None] == eids, base[:, None, None], 0), axis=0)).astype(I32)
    pos_tiles = _tile_pos(pos, tt)
    fill_lo = (base + cnt).astype(I32)
    fill_hi = (tile_end * te).astype(I32)

    xs = _dispatch(xn, pos_tiles, fill_lo, fill_hi, n_valid.reshape(1).astype(I32), n_rows)

    bgu = b_gate_up[0].astype(F32)
    ys = _experts(xs, tile_blk, tile_exp, tile_valid, tile_first, w_gate_up[0],
                  bgu[:, None, 0::2], bgu[:, None, 1::2], w_down[0], b_down[0][:, None, :].astype(F32),
                  _split_perm())

    out = _combine(pos_tiles, h1, gates.T, p[0].reshape(n, -1), ple_proj[0].astype(BF16),
                   ple_norm[0].reshape(1, d), ple_gate[0].astype(BF16), final_norm.reshape(1, d), ys)
    return out.reshape(batch, seq, d)
```

```python
import functools
import math

import numpy as np
import jax
import jax.numpy as jnp
from jax import lax
from jax.experimental import pallas as pl
from jax.experimental.pallas import tpu as pltpu

F32 = jnp.float32
BF16 = jnp.bfloat16
I32 = jnp.int32

HEAD_DIM = 64
DIFF_V_DIM = 2 * HEAD_DIM
NUM_BUCKETS = 32
MAX_DISTANCE = 128
TOP_K = 4
SWIGLU_LIMIT = 7.0
SWIGLU_ALPHA = 1.702
NORM_EPS = 1e-6
LOG2E = math.log2(math.e)

LANES = 128
SUBLANES = 8
MXU_DIM = 256
ATTN_TILE = 256
SB_BLOCK = 256
TOKEN_TILE = 512
INPROJ_TILE = 512
EXPERT_TILE = 512
ISSUE_UNROLL = 4
VMEM_LIMIT = 48 * 1024 * 1024
EXPERT_VMEM_LIMIT = 56 * 1024 * 1024


def _cparams(sem, vmem=None):
    return pltpu.CompilerParams(dimension_semantics=sem, vmem_limit_bytes=vmem)


def _inproj_kernel(x_ref, g_ref, w_ref, o_ref):
    x = x_ref[...]
    inv = lax.rsqrt(jnp.mean(x * x, axis=-1, keepdims=True) + NORM_EPS)
    hn = (x * inv * g_ref[...]).astype(BF16)
    o_ref[...] = jnp.dot(hn, w_ref[...], preferred_element_type=F32).astype(o_ref.dtype)


def _inproj(x2, g, w):
    n, d = x2.shape
    width = w.shape[1]
    tm = INPROJ_TILE
    return pl.pallas_call(
        _inproj_kernel,
        grid=(n // tm,),
        in_specs=[pl.BlockSpec((tm, d), lambda i: (i, 0)),
                  pl.BlockSpec((1, d), lambda i: (0, 0)),
                  pl.BlockSpec((d, width), lambda i: (0, 0))],
        out_specs=pl.BlockSpec((tm, width), lambda i: (i, 0)),
        out_shape=jax.ShapeDtypeStruct((n, width), BF16),
        compiler_params=_cparams(("arbitrary",), VMEM_LIMIT),
        name="inproj",
    )(x2, g, w)


_NT = (((1,), (1,)), ((), ()))


def _attn_kernel(lq1_ref, lk1_ref, lq2_ref, lk2_ref, subln_ref, bias_ref, u_ref,
                 dq1_ref, dq2_ref, dk1_ref, dk2_ref, dv_ref, sq_ref, sk_ref, sv_ref,
                 od_ref, os_ref, s_scr, t_scr, rt_scr, *, tq, nq, lambda_init):
    qi = pl.program_id(2)
    lane = lax.broadcasted_iota(I32, (tq, LANES), 1)
    zq = jnp.zeros((tq, LANES), dq1_ref.dtype)
    dq1 = dq1_ref[...]
    dq2 = dq2_ref[...]
    rows = []
    for hh in range(2):
        mine = (lane >= HEAD_DIM) == (hh == 1)
        rows.append(jnp.concatenate([jnp.where(mine, dq1, zq), zq], axis=1))
        rows.append(jnp.concatenate([zq, jnp.where(mine, dq2, zq)], axis=1))
    qq = jnp.concatenate(rows, axis=0)

    sq = sq_ref[...]
    n_sb = sq.shape[1] // HEAD_DIM
    lane_s = lax.broadcasted_iota(I32, sq.shape, 1)
    zs = jnp.zeros_like(sq)
    qs = jnp.concatenate([jnp.where((lane_s >= hh * HEAD_DIM) & (lane_s < (hh + 1) * HEAD_DIM), sq, zs)
                          for hh in range(n_sb)], axis=0)
    row = lax.broadcasted_iota(I32, (n_sb * tq, tq), 0)
    col = lax.broadcasted_iota(I32, (n_sb * tq, tq), 1)
    strict = col < (row & (tq - 1))
    u = u_ref[...]

    odd = (qi % 2) == 1
    n_loop = qi // 2
    j_top = 2 * n_loop - 1

    def pass1_tile(j, mx, diag):
        start = pl.multiple_of(j * tq, tq)
        n = lax.dot_general(qs, sk_ref[pl.ds(start, tq), :], _NT, preferred_element_type=F32)
        lk = jnp.minimum(n, 0.0) - jnp.log2(1.0 + jnp.exp2(-jnp.abs(n)))
        if diag:
            lk = jnp.where(strict, lk, 0.0)
        sfx = jnp.dot(lk.astype(BF16), u, preferred_element_type=F32)
        t_scr[j] = (lk - n) + sfx
        rt_scr[j] = sfx[:, :1] + lk[:, :1]
        bi = jnp.minimum(qi - j, 2)
        ba = bias_ref[0, bi]
        bb = bias_ref[1, bi]
        kk = jnp.concatenate([dk1_ref[pl.ds(start, tq), :], dk2_ref[pl.ds(start, tq), :]], axis=1)
        s = (lax.dot_general(qq, kk, _NT, preferred_element_type=F32)
             + jnp.concatenate([ba, ba, bb, bb], axis=0))
        s_scr[j] = s
        return jnp.maximum(mx, jnp.maximum(s[:, :LANES], s[:, LANES:]))

    def pass1_pair(g, mx):
        ja = j_top - 2 * g
        return pass1_tile(ja - 1, pass1_tile(ja, mx, False), False)

    mx = jnp.full((4 * tq, LANES), -jnp.inf, F32)
    mx = lax.cond(odd, lambda c: pass1_tile(qi - 1, pass1_tile(qi, c, True), False),
                  lambda c: pass1_tile(qi, c, True), mx)
    mx = lax.fori_loop(0, n_loop, pass1_pair, mx)
    m = jnp.max(mx, axis=1, keepdims=True)

    def pass2_tile(j, state, diag):
        carry, acc_s, l_acc, acc_d = state
        start = pl.multiple_of(j * tq, tq)
        w = jnp.exp2(t_scr[j] + carry)
        if diag:
            w = jnp.where(strict, w, 0.0)
        acc_s = acc_s + jnp.dot(w.astype(BF16), sv_ref[pl.ds(start, tq), :], preferred_element_type=F32)
        carry = carry + rt_scr[j]
        p = jnp.exp2(s_scr[j] - m)
        l_acc = l_acc + (p[:, :LANES] + p[:, LANES:])
        acc_d = acc_d + jnp.dot(p.astype(BF16), dv_ref[pl.ds(start, tq), :], preferred_element_type=F32)
        return carry, acc_s, l_acc, acc_d

    def pass2_pair(g, state):
        ja = j_top - 2 * g
        return pass2_tile(ja - 1, pass2_tile(ja, state, False), False)

    state = (jnp.zeros((n_sb * tq, 1), F32), jnp.zeros((n_sb * tq, sq.shape[1]), F32),
             jnp.zeros((4 * tq, LANES), F32), jnp.zeros((4 * tq, 2 * LANES), F32))
    state = lax.cond(odd, lambda st: pass2_tile(qi - 1, pass2_tile(qi, st, True), False),
                     lambda st: pass2_tile(qi, st, True), state)
    _, acc_s, l_acc, acc_d = lax.fori_loop(0, n_loop, pass2_pair, state)

    out = acc_s[:tq]
    for hh in range(1, n_sb):
        out = jnp.where(lane_s >= hh * HEAD_DIM, acc_s[hh * tq:(hh + 1) * tq], out)
    os_ref[...] = out.astype(os_ref.dtype)

    l = jnp.sum(l_acc, axis=1, keepdims=True)
    lam = (jnp.exp(jnp.sum(lq1_ref[...] * lk1_ref[...], axis=1, keepdims=True))
           - jnp.exp(jnp.sum(lq2_ref[...] * lk2_ref[...], axis=1, keepdims=True)) + lambda_init)
    outs = []
    for hh in range(2):
        r1, r2 = 2 * hh * tq, (2 * hh + 1) * tq
        a1 = acc_d[r1:r1 + tq, hh * LANES:(hh + 1) * LANES]
        a2 = acc_d[r2:r2 + tq, hh * LANES:(hh + 1) * LANES]
        o = a1 / l[r1:r1 + tq] - lam * (a2 / l[r2:r2 + tq])
        o = o * lax.rsqrt(jnp.mean(o * o, axis=-1, keepdims=True) + NORM_EPS) * subln_ref[...]
        outs.append((o * (1.0 - lambda_init)).astype(od_ref.dtype))
    od_ref[...] = jnp.concatenate(outs, axis=1)


def _attention(proj, bias_tiles, u, lq1, lk1, lq2, lk2, subln, *, batch, seq, n_diff, n_sb, sb_col0, lambda_init):
    n = proj.shape[0]
    tq = ATTN_TILE
    nq = seq // tq
    nb = n_diff // 2
    nblk = n_sb * HEAD_DIM // SB_BLOCK
    assert nq % 2 == 0 and nb == nblk
    small = lambda shape: pl.BlockSpec(shape, lambda b, h, i: (0,) * len(shape))
    qspec = lambda grp: pl.BlockSpec((tq, LANES), lambda b, h, i: (b * nq + i, grp * nb + h))
    kspec = lambda grp: pl.BlockSpec((seq, LANES), lambda b, h, i: (b, grp * nb + h))
    sspec = lambda rows, grp: pl.BlockSpec(
        (rows, SB_BLOCK), lambda b, h, i: ((b * nq + i) if rows == tq else b, sb_col0 + grp * nblk + h))
    kern = functools.partial(_attn_kernel, tq=tq, nq=nq, lambda_init=lambda_init)
    return pl.pallas_call(
        kern,
        grid=(batch, nb, nq),
        in_specs=[small((1, HEAD_DIM)), small((1, HEAD_DIM)), small((1, HEAD_DIM)), small((1, HEAD_DIM)),
                  small((1, DIFF_V_DIM)),
                  pl.BlockSpec((2, 3, tq, tq), lambda b, h, i: (h, 0, 0, 0)),
                  pl.BlockSpec((tq, tq), lambda b, h, i: (0, 0)),
                  qspec(0), qspec(1), kspec(2), kspec(3),
                  pl.BlockSpec((seq, 2 * LANES), lambda b, h, i: (b, 2 * nb + h)),
                  sspec(tq, 0), sspec(seq, 1), sspec(seq, 2)],
        out_specs=[pl.BlockSpec((tq, 2 * LANES), lambda b, h, i: (b * nq + i, h)),
                   pl.BlockSpec((tq, SB_BLOCK), lambda b, h, i: (b * nq + i, h))],
        out_shape=[jax.ShapeDtypeStruct((n, n_diff * DIFF_V_DIM), BF16),
                   jax.ShapeDtypeStruct((n, n_sb * HEAD_DIM), BF16)],
        scratch_shapes=[pltpu.VMEM((nq, 4 * tq, tq), F32),
                        pltpu.VMEM((nq, (SB_BLOCK // HEAD_DIM) * tq, tq), F32),
                        pltpu.VMEM((nq, (SB_BLOCK // HEAD_DIM) * tq, 1), F32)],
        compiler_params=_cparams(("arbitrary", "arbitrary", "arbitrary"), EXPERT_VMEM_LIMIT),
        name="attention",
    )(lq1, lk1, lq2, lk2, subln, bias_tiles, u, proj, proj, proj, proj, proj, proj, proj, proj)


def _outproj_router_kernel(od_ref, os_ref, x_ref, wod_ref, wos_ref, g_ref, rwh_ref, rwl_ref, rb_ref, ut_ref,
                           h_ref, xn_ref, ti_ref, gt_ref, rk_ref, cnt_ref, cnt_scr, *, n_experts):
    @pl.when(pl.program_id(0) == 0)
    def _():
        cnt_scr[...] = jnp.zeros_like(cnt_scr)

    attn = (jnp.dot(od_ref[...], wod_ref[...], preferred_element_type=F32)
            + jnp.dot(os_ref[...], wos_ref[...], preferred_element_type=F32))
    h = x_ref[...] + attn
    h_ref[...] = h
    xn = h * lax.rsqrt(jnp.mean(h * h, axis=-1, keepdims=True) + NORM_EPS) * g_ref[...]
    xn_ref[...] = xn
    xh = xn.astype(BF16)
    xl = (xn - xh.astype(F32)).astype(BF16)
    rwh = rwh_ref[...]
    logits = (lax.dot_general(rwh, xh, _NT, preferred_element_type=F32)
              + lax.dot_general(rwh, xl, _NT, preferred_element_type=F32)
              + lax.dot_general(rwl_ref[...], xh, _NT, preferred_element_type=F32)
              + rb_ref[...])
    eidx = lax.broadcasted_iota(I32, logits.shape, 0)
    work = logits
    vals, idxs, sels = [], [], []
    for _ in range(TOP_K):
        mx = jnp.max(work, axis=0, keepdims=True)
        ix = jnp.min(jnp.where(work == mx, eidx, n_experts), axis=0, keepdims=True)
        sel = eidx == ix
        work = jnp.where(sel, -jnp.inf, work)
        vals.append(mx)
        idxs.append(ix)
        sels.append(sel)
    ex = [jnp.exp(v - vals[0]) for v in vals]
    den = ex[0] + ex[1] + ex[2] + ex[3]
    ti_ref[...] = jnp.concatenate(idxs, axis=0)
    gt_ref[...] = jnp.concatenate([e / den for e in ex], axis=0)
    onehot = jnp.where(sels[0] | sels[1] | sels[2] | sels[3], 1.0, 0.0)
    rank = jnp.dot(onehot.astype(BF16), ut_ref[...], preferred_element_type=F32) + cnt_scr[...]
    rk_ref[...] = jnp.concatenate(
        [jnp.sum(jnp.where(s, rank, 0.0), axis=0, keepdims=True) for s in sels], axis=0).astype(I32)
    cnt_scr[...] += jnp.sum(onehot, axis=1, keepdims=True)
    cnt_ref[...] = cnt_scr[...].astype(I32)


def _outproj_router(o_diff, o_sb, x2, wod, wos, g, rwh, rwl, rb, ut):
    n, d = x2.shape
    e = rwh.shape[0]
    tm = TOKEN_TILE
    row = lambda w: pl.BlockSpec((tm, w), lambda i: (i, 0))
    full = lambda a: pl.BlockSpec(a.shape, lambda i: (0,) * a.ndim)
    kcol = pl.BlockSpec((TOP_K, tm), lambda i: (0, i))
    kern = functools.partial(_outproj_router_kernel, n_experts=e)
    return pl.pallas_call(
        kern,
        grid=(n // tm,),
        in_specs=[row(o_diff.shape[1]), row(o_sb.shape[1]), row(d), full(wod), full(wos), full(g),
                  full(rwh), full(rwl), full(rb), full(ut)],
        out_specs=[row(d), row(d), kcol, kcol, kcol, pl.BlockSpec((e, 1), lambda i: (0, 0))],
        out_shape=[jax.ShapeDtypeStruct((n, d), F32), jax.ShapeDtypeStruct((n, d), F32),
                   jax.ShapeDtypeStruct((TOP_K, n), I32), jax.ShapeDtypeStruct((TOP_K, n), F32),
                   jax.ShapeDtypeStruct((TOP_K, n), I32), jax.ShapeDtypeStruct((e, 1), I32)],
        scratch_shapes=[pltpu.VMEM((e, 1), F32)],
        compiler_params=_cparams(("arbitrary",), VMEM_LIMIT),
        name="outproj_router",
    )(o_diff, o_sb, x2, wod, wos, g, rwh, rwl, rb, ut)


def _to_token_major(dst_ref, x, n_tok):
    nc = x.shape[1] // LANES
    for c in range(nc):
        dst_ref[pl.ds(c, n_tok, stride=nc), :] = x[:, c * LANES:(c + 1) * LANES]


def _from_token_major(src_ref, n_tok, nc):
    return jnp.concatenate([src_ref[pl.ds(c, n_tok, stride=nc), :] for c in range(nc)], axis=1)


def _row_copy(src_ref, src_row, dst_ref, dst_row, sem, nc):
    return pltpu.make_async_copy(src_ref.at[pl.ds(pl.multiple_of(src_row * nc, nc), nc), :],
                                 dst_ref.at[pl.ds(pl.multiple_of(dst_row * nc, nc), nc), :], sem)


def _dispatch_kernel(fill_lo_ref, fill_hi_ref, nvalid_ref, pos_ref, xn_ref, xs_ref, tok_scr, zero_scr, sems,
                     *, tm, te, nc, n_experts, n_tiles):
    sem, tail_sem, fill_sem = sems.at[0], sems.at[1], sems.at[2]
    _to_token_major(tok_scr, xn_ref[...], tm)

    def issue(g, c):
        for u in range(ISSUE_UNROLL):
            r = g * ISSUE_UNROLL + u
            for k in range(TOP_K):
                _row_copy(tok_scr, r, xs_ref, pos_ref[0, 0, k * tm + r], sem, nc).start(priority=k % 2)
        return c

    lax.fori_loop(0, tm // ISSUE_UNROLL, issue, 0)

    @pl.when(pl.program_id(0) == 0)
    def _():
        zero_scr[...] = jnp.zeros_like(zero_scr)

        def tail_copy(t):
            return pltpu.make_async_copy(
                zero_scr, xs_ref.at[pl.ds(pl.multiple_of(t * te * nc, te * nc), te * nc), :], tail_sem)

        def tail(t, c):
            tail_copy(t).start()
            return c

        lax.fori_loop(nvalid_ref[0], n_tiles, tail, 0)

        def drain_tail(t, c):
            tail_copy(t).wait()
            return c

        lax.fori_loop(nvalid_ref[0], n_tiles, drain_tail, 0)
        for e in range(n_experts):
            def fill(r, c):
                _row_copy(zero_scr, 0, xs_ref, r, fill_sem, nc).start()
                return c
            lax.fori_loop(fill_lo_ref[e], fill_hi_ref[e], fill, 0)
        for e in range(n_experts):
            def drain_fill(r, c):
                _row_copy(zero_scr, 0, xs_ref, r, fill_sem, nc).wait()
                return c
            lax.fori_loop(fill_lo_ref[e], fill_hi_ref[e], drain_fill, 0)

    for k in range(TOP_K):
        pltpu.make_async_copy(tok_scr, xs_ref.at[pl.ds(0, tm * nc), :], sem).wait()


def _dispatch(xn, pos_tiles, fill_lo, fill_hi, n_valid, n_rows):
    n, d = xn.shape
    tm = TOKEN_TILE
    te = EXPERT_TILE
    nc = d // LANES
    assert nc % SUBLANES == 0
    e = fill_lo.shape[0]
    kern = functools.partial(_dispatch_kernel, tm=tm, te=te, nc=nc, n_experts=e, n_tiles=n_rows // te)
    grid_spec = pltpu.PrefetchScalarGridSpec(
        num_scalar_prefetch=3,
        grid=(n // tm,),
        in_specs=[pl.BlockSpec((1, 1, TOP_K * tm), lambda i, lo, hi, nv: (i, 0, 0), memory_space=pltpu.SMEM),
                  pl.BlockSpec((tm, d), lambda i, lo, hi, nv: (i, 0))],
        out_specs=pl.BlockSpec(memory_space=pl.ANY),
        scratch_shapes=[pltpu.VMEM((tm * nc, LANES), F32), pltpu.VMEM((te * nc, LANES), F32),
                        pltpu.SemaphoreType.DMA((3,))],
    )
    return pl.pallas_call(
        kern,
        grid_spec=grid_spec,
        out_shape=jax.ShapeDtypeStruct((n_rows * nc, LANES), F32),
        compiler_params=_cparams(("arbitrary",)),
        name="dispatch",
    )(fill_lo, fill_hi, n_valid, pos_tiles, xn)


def _expert_kernel(blk_ref, exp_ref, valid_ref, first_ref, xs_ref, wgu_ref, bg_ref, bl_ref, wd_ref, bd_ref,
                   perm_ref, o_ref, wg_scr, wl_scr, wd_scr, *, te):
    i = pl.program_id(0)
    nc = wd_ref.shape[2] // LANES
    valid = valid_ref[i] == 1

    @pl.when(first_ref[i] == 1)
    def _():
        perm = perm_ref[...]
        for g in range(wgu_ref.shape[2] // MXU_DIM):
            blk = wgu_ref[0, :, g * MXU_DIM:(g + 1) * MXU_DIM].astype(BF16)
            sp = jnp.dot(blk, perm, preferred_element_type=F32).astype(BF16)
            wg_scr[:, g * LANES:(g + 1) * LANES] = sp[:, :LANES]
            wl_scr[:, g * LANES:(g + 1) * LANES] = sp[:, LANES:]
        wd_scr[...] = wd_ref[0].astype(BF16)

    @pl.when(jnp.logical_not(valid))
    def _():
        o_ref[...] = jnp.zeros_like(o_ref)

    @pl.when(valid)
    def _():
        x = _from_token_major(xs_ref, te, nc).astype(BF16)
        glu = jnp.dot(x, wg_scr[...], preferred_element_type=F32) + bg_ref[0]
        lin = jnp.dot(x, wl_scr[...], preferred_element_type=F32) + bl_ref[0]
        glu = jnp.minimum(glu, SWIGLU_LIMIT)
        lin = jnp.clip(lin, -SWIGLU_LIMIT, SWIGLU_LIMIT)
        act = glu * jax.nn.sigmoid(SWIGLU_ALPHA * glu) * (lin + 1.0)
        out = jnp.dot(act.astype(BF16), wd_scr[...], preferred_element_type=F32) + bd_ref[0]
        _to_token_major(o_ref, out, te)


def _experts(xs, tile_blk, tile_exp, tile_valid, tile_first, wgu, bg, bl, wd, bd, perm):
    de, d = wd.shape[1], wd.shape[2]
    nc = d // LANES
    te = EXPERT_TILE
    n_tiles = xs.shape[0] // (te * nc)
    per_e = lambda shape: pl.BlockSpec((1,) + shape, lambda i, blk, ex, va, fi: (ex[i], 0, 0))
    grid_spec = pltpu.PrefetchScalarGridSpec(
        num_scalar_prefetch=4,
        grid=(n_tiles,),
        in_specs=[pl.BlockSpec((te * nc, LANES), lambda i, blk, ex, va, fi: (blk[i], 0)),
                  per_e((d, 2 * de)), per_e((1, de)), per_e((1, de)), per_e((de, d)), per_e((1, d)),
                  pl.BlockSpec((MXU_DIM, MXU_DIM), lambda i, blk, ex, va, fi: (0, 0))],
        out_specs=pl.BlockSpec((te * nc, LANES), lambda i, blk, ex, va, fi: (i, 0)),
        scratch_shapes=[pltpu.VMEM((d, de), BF16), pltpu.VMEM((d, de), BF16), pltpu.VMEM((de, d), BF16)],
    )
    return pl.pallas_call(
        functools.partial(_expert_kernel, te=te),
        grid_spec=grid_spec,
        out_shape=jax.ShapeDtypeStruct(xs.shape, F32),
        compiler_params=_cparams(("arbitrary",), EXPERT_VMEM_LIMIT),
        name="experts",
    )(tile_blk, tile_exp, tile_valid, tile_first, xs, wgu, bg, bl, wd, bd, perm)


def _combine_kernel(pos_ref, next_pos_ref, h_ref, gt_ref, p_ref, wp_ref, gp_ref, wg_ref, gf_ref, ys_ref, o_ref,
                    rows_scr, sems, *, tm, n_steps):
    i = pl.program_id(0)
    slot = i % 2
    nc = h_ref.shape[1] // LANES

    def start_gather(idx_ref, s):
        def body(g, c):
            for u in range(ISSUE_UNROLL):
                r = g * ISSUE_UNROLL + u
                for k in range(TOP_K):
                    _row_copy(ys_ref, idx_ref[0, 0, k * tm + r], rows_scr.at[s, k], r, sems.at[s],
                              nc).start(priority=k % 2)
            return c
        lax.fori_loop(0, tm // ISSUE_UNROLL, body, 0)

    @pl.when(i == 0)
    def _():
        start_gather(pos_ref, slot)

    @pl.when(i + 1 < n_steps)
    def _():
        start_gather(next_pos_ref, 1 - slot)

    emb = jnp.dot(p_ref[...].astype(BF16), wp_ref[...], preferred_element_type=F32)
    emb = emb * lax.rsqrt(jnp.mean(emb * emb, axis=-1, keepdims=True) + NORM_EPS) * gp_ref[...]

    for k in range(TOP_K):
        pltpu.make_async_copy(ys_ref.at[pl.ds(0, tm * nc), :], rows_scr.at[slot, k], sems.at[slot]).wait()

    gates = gt_ref[...]
    y = gates[:, 0:1] * _from_token_major(rows_scr.at[slot, 0], tm, nc)
    for k in range(1, TOP_K):
        y = y + gates[:, k:k + 1] * _from_token_major(rows_scr.at[slot, k], tm, nc)
    h = h_ref[...] + y
    gate = jax.nn.sigmoid(jnp.dot(h.astype(BF16), wg_ref[...], preferred_element_type=F32))
    h = h + gate * emb
    o_ref[...] = h * lax.rsqrt(jnp.mean(h * h, axis=-1, keepdims=True) + NORM_EPS) * gf_ref[...]


def _combine(pos_tiles, h1, gates_t, p2, wp, gp, wg, gf, ys):
    n, d = h1.shape
    tm = TOKEN_TILE
    row = lambda w: pl.BlockSpec((tm, w), lambda i: (i, 0))
    full = lambda a: pl.BlockSpec(a.shape, lambda i: (0,) * a.ndim)
    n_steps = n // tm
    kern = functools.partial(_combine_kernel, tm=tm, n_steps=n_steps)
    pos_spec = lambda off: pl.BlockSpec((1, 1, TOP_K * tm), lambda i: (jnp.minimum(i + off, n_steps - 1), 0, 0),
                                        memory_space=pltpu.SMEM)
    return pl.pallas_call(
        kern,
        grid=(n_steps,),
        in_specs=[pos_spec(0), pos_spec(1),
                  row(d), row(TOP_K), row(p2.shape[1]), full(wp), full(gp), full(wg), full(gf),
                  pl.BlockSpec(memory_space=pl.ANY)],
        out_specs=row(d),
        out_shape=jax.ShapeDtypeStruct((n, d), F32),
        scratch_shapes=[pltpu.VMEM((2, TOP_K, tm * (d // LANES), LANES), F32), pltpu.SemaphoreType.DMA((2,))],
        compiler_params=_cparams(("arbitrary",), VMEM_LIMIT),
        name="combine_ple",
    )(pos_tiles, pos_tiles, h1, gates_t, p2, wp, gp, wg, gf, ys)


def _t5_bucket(rel):
    n = jnp.maximum(rel, 0)
    max_exact = NUM_BUCKETS // 2
    nf = jnp.maximum(n, 1).astype(F32)
    large = max_exact + (jnp.log(nf / max_exact) / math.log(MAX_DISTANCE / max_exact)
                         * (NUM_BUCKETS - max_exact)).astype(I32)
    large = jnp.minimum(large, NUM_BUCKETS - 1)
    return jnp.where(n < max_exact, n, large)


def _bias_tiles(rel_bias, tq):
    n_heads = rel_bias.shape[1]
    lo = -tq
    rel = jnp.arange(lo, 3 * tq)
    vec = rel_bias.astype(F32)[_t5_bucket(rel)] * LOG2E
    vec = jnp.where((rel >= 0)[:, None], vec, -jnp.inf).T
    length = vec.shape[1]
    rev = vec[:, ::-1]
    tiles = []
    for d in range(3):
        c = d * tq - lo
        o = length - c - tq
        win = rev[:, o:o + 2 * tq]
        skew = jnp.tile(win, (1, tq))[:, :tq * (2 * tq - 1)].reshape(n_heads, tq, 2 * tq - 1)
        tiles.append(skew[:, :, tq - 1:])
    return jnp.stack(tiles, axis=1)


def _far_bias_is_constant(tq):
    n = np.arange(tq + 1, 4 * tq).astype(np.float32)
    max_exact = NUM_BUCKETS // 2
    large = max_exact + (np.log(n / max_exact) / math.log(MAX_DISTANCE / max_exact)
                         * (NUM_BUCKETS - max_exact)).astype(np.int32)
    return bool(np.all(np.minimum(large, NUM_BUCKETS - 1) == NUM_BUCKETS - 1))


def _tile_pos(pos, tm):
    k, n = pos.shape
    return pos.reshape(k, n // tm, tm).transpose(1, 0, 2).reshape(n // tm, 1, k * tm)


def _split_perm():
    src = np.arange(MXU_DIM)
    dst = np.where(src % 2 == 0, src // 2, LANES + src // 2)
    perm = np.zeros((MXU_DIM, MXU_DIM), np.float32)
    perm[src, dst] = 1.0
    return jnp.asarray(perm, BF16)


def kernel(x, p, w_in, w_out, attn_norm, moe_norm, rel_bias, lambda_q1, lambda_k1, lambda_q2, lambda_k2, subln,
           router_w, router_b, w_gate_up, b_gate_up, w_down, b_down, ple_proj, ple_norm, ple_gate, final_norm):
    batch, seq, d = x.shape
    assert w_in.shape[0] == 1
    n = batch * seq
    n_experts = router_w.shape[2]
    n_diff = (d // 2) // (2 * HEAD_DIM)
    n_sb = (d - d // 2) // HEAD_DIM
    qk_w = n_diff * HEAD_DIM
    sb0 = 4 * qk_w + n_diff * DIFF_V_DIM
    assert 2 * HEAD_DIM == LANES and (n_sb * HEAD_DIM) % SB_BLOCK == 0 and n_diff % 2 == 0
    assert sb0 % SB_BLOCK == 0 and ATTN_TILE & (ATTN_TILE - 1) == 0
    assert seq % ATTN_TILE == 0 and n % INPROJ_TILE == 0 and n % TOKEN_TILE == 0
    assert _far_bias_is_constant(ATTN_TILE)
    lambda_init = 0.8 - 0.6 * math.exp(-0.3 * 0)
    scale = HEAD_DIM ** -0.5

    x2 = x.reshape(n, d)
    col_scale = np.ones((w_in.shape[2],), np.float32)
    col_scale[:2 * qk_w] = scale * LOG2E
    col_scale[sb0:sb0 + n_sb * HEAD_DIM] = -scale * LOG2E
    w_all = (w_in[0] * col_scale[None, :]).astype(BF16)

    proj = _inproj(x2, attn_norm[0].reshape(1, d), w_all)

    t = ATTN_TILE
    u = (jnp.arange(t)[:, None] > jnp.arange(t)[None, :]).astype(BF16)
    o_diff, o_sb = _attention(
        proj, _bias_tiles(rel_bias, ATTN_TILE), u,
        lambda_q1[0].reshape(1, HEAD_DIM).astype(F32), lambda_k1[0].reshape(1, HEAD_DIM).astype(F32),
        lambda_q2[0].reshape(1, HEAD_DIM).astype(F32), lambda_k2[0].reshape(1, HEAD_DIM).astype(F32),
        subln[0].reshape(1, DIFF_V_DIM).astype(F32),
        batch=batch, seq=seq, n_diff=n_diff, n_sb=n_sb, sb_col0=sb0 // SB_BLOCK, lambda_init=lambda_init)

    wo = w_out[0].astype(BF16)
    rw = router_w[0].T
    rwh = rw.astype(BF16)
    rwl = (rw - rwh.astype(F32)).astype(BF16)
    tt = TOKEN_TILE
    ut = (jnp.arange(tt)[:, None] < jnp.arange(tt)[None, :]).astype(BF16)
    h1, xn, top_i, gates, rank, counts = _outproj_router(
        o_diff, o_sb, x2, wo[:n_diff * DIFF_V_DIM], wo[n_diff * DIFF_V_DIM:], moe_norm[0].reshape(1, d),
        rwh, rwl, router_b[0].reshape(n_experts, 1).astype(F32), ut)

    te = EXPERT_TILE
    n_rows = n * TOP_K + n_experts * te
    n_tiles = n_rows // te
    cnt = counts[:, 0]
    tiles_e = (cnt + te - 1) // te
    tile_end = jnp.cumsum(tiles_e)
    tile_start = tile_end - tiles_e
    base = tile_start * te
    n_valid = tile_end[-1]
    tidx = jnp.arange(n_tiles, dtype=I32)
    tile_valid = (tidx < n_valid).astype(I32)
    tile_blk = jnp.minimum(tidx, n_valid - 1).astype(I32)
    tile_exp = jnp.minimum(jnp.sum(tile_blk[:, None] >= tile_end[None, :], axis=1), n_experts - 1).astype(I32)
    tile_first = (jnp.any((tidx[:, None] == tile_start[None, :]) & (tiles_e[None, :] > 0), axis=1)
                  & (tidx < n_valid)).astype(I32)
    eids = jnp.arange(n_experts, dtype=I32)[:, None, None]
    pos = (rank + jnp.sum(jnp.where(top_i[None] == eids, base[:, None, None], 0), axis=0)).astype(I32)
    pos_tiles = _tile_pos(pos, tt)
    fill_lo = (base + cnt).astype(I32)
    fill_hi = (tile_end * te).astype(I32)

    xs = _dispatch(xn, pos_tiles, fill_lo, fill_hi, n_valid.reshape(1).astype(I32), n_rows)

    bgu = b_gate_up[0].astype(F32)
    ys = _experts(xs, tile_blk, tile_exp, tile_valid, tile_first, w_gate_up[0],
                  bgu[:, None, 0::2], bgu[:, None, 1::2], w_down[0], b_down[0][:, None, :].astype(F32),
                  _split_perm())

    out = _combine(pos_tiles, h1, gates.T, p[0].reshape(n, -1), ple_proj[0].astype(BF16),
                   ple_norm[0].reshape(1, d), ple_gate[0].astype(BF16), final_norm.reshape(1, d), ys)
    return out.reshape(batch, seq, d)
```
